```python
import jax, jax.numpy as jnp
from jax import lax
import numpy as np

D_MODEL = 2048
BATCH = 1
SEQ = 8192
DEPTH = 1

MIX_WIDTH = D_MODEL
MLSTM_HEADS = 4
MLSTM_WIDTH = MIX_WIDTH // 2
MLSTM_DV = MLSTM_WIDTH // MLSTM_HEADS
MLSTM_DQK = MLSTM_DV // 2
MLSTM_CHUNK = 64
GATE_SOFTCAP = 15.0
FOX_HEAD_DIM = 128
FOX_WIDTH = MIX_WIDTH - MLSTM_WIDTH
FOX_HEADS = FOX_WIDTH // FOX_HEAD_DIM
FOX_BLOCK = 128
D_IN_PROJ = 2 * MLSTM_HEADS * MLSTM_DQK + 2 * MLSTM_WIDTH + 2 * MLSTM_HEADS + 3 * FOX_WIDTH + FOX_HEADS
N_EXPERTS = 32
TOP_K = 4
D_FF = D_MODEL
SWIGLU_LIMIT = 7.0
SWIGLU_ALPHA = 1.702
MOE_BLOCK = 128
ADA_SCALE = 0.5
EPS = 1e-6

kernel_name = 'hymba_mlstm_fox_moe_adaln_block'


def rms_last(a, g):
    a = a.astype(jnp.float32)
    return a * lax.rsqrt(jnp.mean(a * a, axis=-1, keepdims=True) + EPS) * g.astype(jnp.float32)


def modulate(x, g, shift, scale):
    return rms_last(x, g) * (1.0 + scale[:, None, :]) + shift[:, None, :]


def soft_cap(a):
    return GATE_SOFTCAP * jnp.tanh(a / GATE_SOFTCAP)


def split_heads(a, n_heads):
    b, s, _ = a.shape
    return a.reshape(b, s, n_heads, -1).transpose(0, 2, 1, 3)


def merge_heads(a):
    b, h, s, d = a.shape
    return a.transpose(0, 2, 1, 3).reshape(b, s, h * d)


def mlstm_chunkwise(q, k, v, i_pre, f_pre):
    b, h, s, _ = q.shape
    L = MLSTM_CHUNK
    nc = s // L
    q = q * (MLSTM_DQK ** -0.5)
    log_f = jax.nn.log_sigmoid(f_pre)

    def to_chunks(a):
        return jnp.moveaxis(a.reshape(b, h, nc, L, *a.shape[3:]), 2, 0)

    causal = jnp.tril(jnp.ones((L, L), dtype=bool))

    def step(carry, inp):
        C, n, m = carry
        qc, kc, vc, ic, fc = inp
        g = jnp.cumsum(fc, axis=-1)
        dmat = jnp.where(causal, g[..., :, None] - g[..., None, :] + ic[..., None, :], -jnp.inf)
        m_inter = g + m[..., None]
        m_t = jnp.maximum(m_inter, jnp.max(dmat, axis=-1))
        scores = jnp.einsum('bhtd,bhsd->bhts', qc, kc) * jnp.exp(dmat - m_t[..., None])
        inter = jnp.exp(m_inter - m_t)
        num = jnp.einsum('bhts,bhsv->bhtv', scores, vc) + inter[..., None] * jnp.einsum('bhtd,bhdv->bhtv', qc, C)
        den = jnp.sum(scores, axis=-1) + inter * jnp.einsum('bhtd,bhd->bht', qc, n)
        h_out = num / jnp.maximum(jnp.abs(den), jnp.exp(-m_t))[..., None]
        g_last = g[..., -1]
        log_w = g_last[..., None] - g + ic
        m_new = jnp.maximum(g_last + m, jnp.max(log_w, axis=-1))
        w = jnp.exp(log_w - m_new[..., None])
        decay = jnp.exp(g_last + m - m_new)
        C = decay[..., None, None] * C + jnp.einsum('bhs,bhsd,bhsv->bhdv', w, kc, vc)
        n = decay[..., None] * n + jnp.einsum('bhs,bhsd->bhd', w, kc)
        return (C, n, m_new), h_out

    init = (jnp.zeros((b, h, MLSTM_DQK, MLSTM_DV), jnp.float32),
            jnp.zeros((b, h, MLSTM_DQK), jnp.float32),
            jnp.zeros((b, h), jnp.float32))
    _, hs = lax.scan(step, init, (to_chunks(q), to_chunks(k), to_chunks(v), to_chunks(i_pre), to_chunks(log_f)))
    return jnp.moveaxis(hs, 0, 2).reshape(b, h, s, MLSTM_DV)


def forgetting_attention(q, k, v, f_pre):
    b, h, s, d = q.shape
    nb = s // FOX_BLOCK
    F = jnp.cumsum(jax.nn.log_sigmoid(f_pre), axis=-1)
    qb = jnp.moveaxis(q.reshape(b, h, nb, FOX_BLOCK, d), 2, 0)
    Fb = jnp.moveaxis(F.reshape(b, h, nb, FOX_BLOCK), 2, 0)
    kpos = jnp.arange(s)
    scale = FOX_HEAD_DIM ** -0.5

    def block(args):
        qi, Fi, bi = args
        qpos = bi * FOX_BLOCK + jnp.arange(FOX_BLOCK)
        logits = jnp.einsum('bhtd,bhsd->bhts', qi, k) * scale + Fi[..., :, None] - F[..., None, :]
        logits = jnp.where(kpos[None, :] <= qpos[:, None], logits, -jnp.inf)
        p = jax.nn.softmax(logits, axis=-1)
        return jnp.einsum('bhts,bhsd->bhtd', p, v)

    out = lax.map(block, (qb, Fb, jnp.arange(nb)))
    return jnp.moveaxis(out, 0, 2).reshape(b, h, s, d)


def moe_ffn(xt, w_router, b_router, w_up_gate, b_up_gate, w_down, b_down):
    t, d = xt.shape
    logits = (xt @ w_router + b_router).astype(jnp.float32)
    top_logits, top_idx = lax.top_k(logits, TOP_K)
    top_w = jax.nn.softmax(top_logits, axis=-1)
    a = t * TOP_K
    flat_e = top_idx.reshape(a)
    flat_tok = jnp.repeat(jnp.arange(t, dtype=jnp.int32), TOP_K)
    flat_w = top_w.reshape(a)
    order = jnp.argsort(flat_e)
    sorted_e = flat_e[order]
    counts = jnp.zeros((N_EXPERTS,), jnp.int32).at[flat_e].add(1)
    n_blocks_e = (counts + MOE_BLOCK - 1) // MOE_BLOCK
    blk_end = jnp.cumsum(n_blocks_e)
    blk_start = blk_end - n_blocks_e
    tok_start = jnp.cumsum(counts) - counts
    rank = jnp.arange(a, dtype=jnp.int32) - tok_start[sorted_e]
    dest = blk_start[sorted_e] * MOE_BLOCK + rank
    nb = -(-a // MOE_BLOCK) + N_EXPERTS
    rows = nb * MOE_BLOCK
    row_tok = jnp.full((rows,), t, jnp.int32).at[dest].set(flat_tok[order])
    row_w = jnp.zeros((rows,), jnp.float32).at[dest].set(flat_w[order])
    block_e = jnp.minimum(jnp.searchsorted(blk_end, jnp.arange(nb), side='right'), N_EXPERTS - 1)
    x_pad = jnp.concatenate([xt, jnp.zeros((1, d), xt.dtype)], axis=0)
    xb = x_pad[row_tok].reshape(nb, MOE_BLOCK, d)

    def expert_block(args):
        xi, e = args
        gu = xi @ w_up_gate[e] + b_up_gate[e]
        gate = jnp.minimum(gu[:, :D_FF], SWIGLU_LIMIT)
        up = jnp.clip(gu[:, D_FF:], -SWIGLU_LIMIT, SWIGLU_LIMIT)
        act = (up + 1.0) * gate * jax.nn.sigmoid(SWIGLU_ALPHA * gate)
        return act @ w_down[e] + b_down[e]

    yb = lax.map(expert_block, (xb, block_e)).reshape(rows, d).astype(jnp.float32)
    out = jnp.zeros((t + 1, d), jnp.float32).at[row_tok].add(yb * row_w[:, None])
    return out[:t]


def setup_inputs(seed: int = 0) -> dict:
    key = jax.random.key(seed)
    ks = jax.random.split(key, 24)
    f32 = jnp.float32
    L, D, E, F = DEPTH, D_MODEL, N_EXPERTS, D_FF

    def nrm(k, shape, scale):
        return jax.random.normal(k, shape, f32) * scale

    return {
        'x': nrm(ks[0], (BATCH, SEQ, D), 1.0),
        'c': nrm(ks[1], (BATCH, D), 1.0),
        'w_ada': nrm(ks[2], (L, D, 6 * D), ADA_SCALE * D ** -0.5),
        'b_ada': nrm(ks[3], (L, 6 * D), 0.02),
        'norm_mix': 1.0 + nrm(ks[4], (L, D), 0.02),
        'w_in': nrm(ks[5], (L, D, D_IN_PROJ), D ** -0.5),
        'b_i': nrm(ks[6], (L, MLSTM_HEADS), 0.1),
        'b_f': jnp.linspace(3.0, 6.0, MLSTM_HEADS, dtype=f32)[None, :] + nrm(ks[7], (L, MLSTM_HEADS), 0.1),
        'fox_b_f': jnp.linspace(1.0, 5.0, FOX_HEADS, dtype=f32)[None, :] + nrm(ks[8], (L, FOX_HEADS), 0.1),
        'fox_q_norm': 1.0 + nrm(ks[9], (L, FOX_HEAD_DIM), 0.02),
        'fox_k_norm': 1.0 + nrm(ks[10], (L, FOX_HEAD_DIM), 0.02),
        'mlstm_out_norm': 1.0 + nrm(ks[11], (L, MLSTM_HEADS, MLSTM_DV), 0.02),
        'fox_out_norm': 1.0 + nrm(ks[12], (L, FOX_HEADS, FOX_HEAD_DIM), 0.02),
        'w_out': nrm(ks[13], (L, MIX_WIDTH, D), MIX_WIDTH ** -0.5),
        'norm_ffn': 1.0 + nrm(ks[14], (L, D), 0.02),
        'w_router': nrm(ks[15], (L, D, E), D ** -0.5),
        'b_router': nrm(ks[16], (L, E), 0.01),
        'w_up_gate': nrm(ks[17], (L, E, D, 2 * F), D ** -0.5),
        'b_up_gate': nrm(ks[18], (L, E, 2 * F), 0.02),
        'w_down': nrm(ks[19], (L, E, F, D), F ** -0.5),
        'b_down': nrm(ks[20], (L, E, D), 0.02),
        'w_ada_final': nrm(ks[21], (D, 2 * D), ADA_SCALE * D ** -0.5),
        'b_ada_final': nrm(ks[22], (2 * D,), 0.02),
        'norm_final': 1.0 + nrm(ks[23], (D,), 0.02),
    }


def reference(x, c, w_ada, b_ada, norm_mix, w_in, b_i, b_f, fox_b_f, fox_q_norm, fox_k_norm,
              mlstm_out_norm, fox_out_norm, w_out, norm_ffn, w_router, b_router, w_up_gate,
              b_up_gate, w_down, b_down, w_ada_final, b_ada_final, norm_final):
    b, s, d = x.shape
    c_act = jax.nn.silu(c.astype(jnp.float32))
    sizes = [MLSTM_HEADS * MLSTM_DQK, MLSTM_HEADS * MLSTM_DQK, MLSTM_WIDTH, MLSTM_WIDTH,
             MLSTM_HEADS, MLSTM_HEADS, FOX_WIDTH, FOX_WIDTH, FOX_WIDTH, FOX_HEADS]
    offsets = []
    acc = 0
    for sz in sizes[:-1]:
        acc += sz
        offsets.append(acc)
    h_res = x.astype(jnp.float32)
    for l in range(DEPTH):
        mod = c_act @ w_ada[l] + b_ada[l]
        sh1, sc1, g1, sh2, sc2, g2 = jnp.split(mod, 6, axis=-1)
        hn = modulate(h_res, norm_mix[l], sh1, sc1)
        proj = (hn @ w_in[l]).astype(jnp.float32)
        mq, mk, mv, mo, mi, mf, fq, fk, fv, ff = jnp.split(proj, offsets, axis=-1)
        i_pre = soft_cap(mi + b_i[l]).transpose(0, 2, 1)
        f_pre = soft_cap(mf + b_f[l]).transpose(0, 2, 1)
        hm = mlstm_chunkwise(split_heads(mq, MLSTM_HEADS), split_heads(mk, MLSTM_HEADS),
                             split_heads(mv, MLSTM_HEADS), i_pre, f_pre)
        hm = merge_heads(rms_last(hm, mlstm_out_norm[l][:, None, :])) * jax.nn.sigmoid(mo)
        qf = rms_last(split_heads(fq, FOX_HEADS), fox_q_norm[l])
        kf = rms_last(split_heads(fk, FOX_HEADS), fox_k_norm[l])
        fox_f = (ff + fox_b_f[l]).transpose(0, 2, 1)
        hf = forgetting_attention(qf, kf, split_heads(fv, FOX_HEADS), fox_f)
        hf = merge_heads(rms_last(hf, fox_out_norm[l][:, None, :]))
        y = jnp.concatenate([hm, hf], axis=-1) @ w_out[l]
        h_res = h_res + g1[:, None, :] * y
        h2 = modulate(h_res, norm_ffn[l], sh2, sc2)
        ffn = moe_ffn(h2.reshape(b * s, d), w_router[l], b_router[l], w_up_gate[l],
                      b_up_gate[l], w_down[l], b_down[l]).reshape(b, s, d)
        h_res = h_res + g2[:, None, :] * ffn
    modf = c_act @ w_ada_final + b_ada_final
    shf, scf = jnp.split(modf, 2, axis=-1)
    out = modulate(h_res, norm_final, shf, scf)
    return out.astype(x.dtype)
```

```python
import functools

import jax
import jax.numpy as jnp
from jax import lax
from jax.experimental import pallas as pl
from jax.experimental.pallas import tpu as pltpu

F32 = jnp.float32
BF16 = jnp.bfloat16
U32 = jnp.uint32
I32 = jnp.int32
NEG_INF = float("-inf")
HIGHEST = lax.Precision.HIGHEST

D_MODEL = 2048
MLSTM_HEADS = 4
MLSTM_DQK = 128
MLSTM_DV = 256
MLSTM_WIDTH = MLSTM_HEADS * MLSTM_DV
GATE_SOFTCAP = 15.0
FOX_HEADS = 8
FOX_HEAD_DIM = 128
FOX_WIDTH = FOX_HEADS * FOX_HEAD_DIM
N_EXPERTS = 32
TOP_K = 4
D_FF = 2048
SWIGLU_LIMIT = 7.0
SWIGLU_ALPHA = 1.702
EPS = 1e-6

LANES = 128
P_WIDTH = 6144
HALF = D_MODEL // 2

ADA_TN = 512
INPROJ_TM = 1024
INPROJ_TN = 512
GATES_TL = 512
MLSTM_L = 256
FOX_BQ = 512
OUT_TM = 256
ROWS_PER_CHUNK = 1280
SUB_ROWS = 256
FF_TILE = 256
PAIRS_PER_STEP = 2048
FINAL_TM = 256
MIB = 1024 * 1024


def _params(n_axes, vmem_mib):
    return pltpu.CompilerParams(dimension_semantics=("arbitrary",) * n_axes,
                                vmem_limit_bytes=vmem_mib * MIB)


def _log_sigmoid(z):
    return jnp.minimum(z, 0.0) - jnp.log1p(jnp.exp(-jnp.abs(z)))


def _sigmoid(z):
    return 1.0 / (1.0 + jnp.exp(-z))


def _ada_kernel(c_ref, w_ref, b_ref, o_ref):
    c = c_ref[...]
    ca = c * _sigmoid(c)
    o_ref[...] = jnp.sum(ca * w_ref[...], axis=0, keepdims=True) + b_ref[...]


def _ada_mod(c_col, w, b_row):
    d, n = w.shape
    return pl.pallas_call(
        _ada_kernel,
        grid=(n // ADA_TN,),
        in_specs=[pl.BlockSpec((d, 1), lambda j: (0, 0)),
                  pl.BlockSpec((d, ADA_TN), lambda j: (0, j)),
                  pl.BlockSpec((1, ADA_TN), lambda j: (0, j))],
        out_specs=pl.BlockSpec((1, ADA_TN), lambda j: (0, j)),
        out_shape=jax.ShapeDtypeStruct((1, n), F32),
        compiler_params=_params(1, 32),
        name="ada_mod",
    )(c_col, w, b_row)


_NORM_TILE_LO = (4 * 1024 + 2 * 512 - 2048) // INPROJ_TN
_NORM_TILE_HI = 5120 // INPROJ_TN


def _inproj_kernel(x_ref, g_ref, sh_ref, sc_ref, w_ref, wg_ref, cg_ref, p_ref, gate_ref, hn_ref):
    j = pl.program_id(1)

    @pl.when(j == 0)
    def _():
        x = x_ref[...]
        ms = jnp.mean(x * x, axis=-1, keepdims=True)
        hn = x * lax.rsqrt(ms + EPS) * g_ref[...] * (1.0 + sc_ref[...]) + sh_ref[...]
        hnb = hn.astype(BF16)
        hn_ref[...] = hnb
        gate_ref[...] = jnp.dot(hnb, wg_ref[...], preferred_element_type=F32)

    acc = jnp.dot(hn_ref[...], w_ref[...], preferred_element_type=F32)
    is_norm = jnp.logical_and(j >= _NORM_TILE_LO, j < _NORM_TILE_HI)

    @pl.when(is_norm)
    def _():
        parts = []
        for hh in range(INPROJ_TN // FOX_HEAD_DIM):
            a = acc[:, hh * FOX_HEAD_DIM:(hh + 1) * FOX_HEAD_DIM]
            parts.append(a * lax.rsqrt(jnp.mean(a * a, axis=-1, keepdims=True) + EPS))
        p_ref[...] = (jnp.concatenate(parts, axis=-1) * cg_ref[...]).astype(BF16)

    @pl.when(jnp.logical_not(is_norm))
    def _():
        p_ref[...] = (acc * cg_ref[...]).astype(BF16)


def _in_proj(x2, g_row, sh_row, sc_row, wp, wg, colgain):
    s, d = x2.shape
    tm, tn = INPROJ_TM, INPROJ_TN
    return pl.pallas_call(
        _inproj_kernel,
        grid=(s // tm, P_WIDTH // tn),
        in_specs=[pl.BlockSpec((tm, d), lambda i, j: (i, 0)),
                  pl.BlockSpec((1, d), lambda i, j: (0, 0)),
                  pl.BlockSpec((1, d), lambda i, j: (0, 0)),
                  pl.BlockSpec((1, d), lambda i, j: (0, 0)),
                  pl.BlockSpec((d, tn), lambda i, j: (0, j)),
                  pl.BlockSpec((d, LANES), lambda i, j: (0, 0)),
                  pl.BlockSpec((1, tn), lambda i, j: (0, j))],
        out_specs=[pl.BlockSpec((tm, tn), lambda i, j: (i, j)),
                   pl.BlockSpec((tm, LANES), lambda i, j: (i, 0))],
        out_shape=[jax.ShapeDtypeStruct((s, P_WIDTH), BF16),
                   jax.ShapeDtypeStruct((s, LANES), F32)],
        scratch_shapes=[pltpu.VMEM((tm, d), BF16)],
        compiler_params=_params(2, 48),
        name="in_proj",
    )(x2, g_row, sh_row, sc_row, wp, wg, colgain)


def _gates_kernel(g_ref, b_ref, a_ref, carry_ref):
    i = pl.program_id(0)

    @pl.when(i == 0)
    def _():
        carry_ref[...] = jnp.zeros_like(carry_ref)

    tl = g_ref.shape[0]
    z = g_ref[...] + b_ref[...]
    col = lax.broadcasted_iota(I32, z.shape, 1)
    capped = GATE_SOFTCAP * jnp.tanh(z / GATE_SOFTCAP)
    r = lax.broadcasted_iota(I32, (tl, tl), 0)
    cc = lax.broadcasted_iota(I32, (tl, tl), 1)
    tril = (cc <= r).astype(F32)
    cs = jnp.dot(tril, _log_sigmoid(z), precision=HIGHEST, preferred_element_type=F32) + carry_ref[...]
    carry_ref[...] = cs[tl - 1:tl, :]
    a_ref[...] = jnp.where(col < MLSTM_HEADS, capped,
                           jnp.where(col < 2 * MLSTM_HEADS, _log_sigmoid(capped), cs))


def _gates(gates, bias_row):
    s = gates.shape[0]
    tl = GATES_TL
    return pl.pallas_call(
        _gates_kernel,
        grid=(s // tl,),
        in_specs=[pl.BlockSpec((tl, LANES), lambda i: (i, 0)),
                  pl.BlockSpec((1, LANES), lambda i: (0, 0))],
        out_specs=pl.BlockSpec((tl, LANES), lambda i: (i, 0)),
        out_shape=jax.ShapeDtypeStruct((s, LANES), F32),
        scratch_shapes=[pltpu.VMEM((1, LANES), F32)],
        compiler_params=_params(1, 32),
        name="gates",
    )(gates, bias_row)


def _mlstm_kernel(q_ref, k_ref, v_ref, mo_ref, a_ref, at_ref, gain_ref, o_ref, c_ref, n_ref, m_ref):
    ci = pl.program_id(0)
    L = MLSTM_L

    @pl.when(ci == 0)
    def _():
        c_ref[...] = jnp.zeros_like(c_ref)
        n_ref[...] = jnp.zeros_like(n_ref)
        m_ref[...] = jnp.zeros_like(m_ref)

    r = lax.broadcasted_iota(I32, (L, L), 0)
    cc = lax.broadcasted_iota(I32, (L, L), 1)
    causal = cc <= r
    tril = causal.astype(F32)
    a = a_ref[...]
    at = at_ref[...]
    g_cols = jnp.dot(tril, a, precision=HIGHEST, preferred_element_type=F32)
    g_rows = jnp.dot(at, (r <= cc).astype(F32), precision=HIGHEST, preferred_element_type=F32)

    for hh in range(MLSTM_HEADS):
        q = q_ref[:, hh * MLSTM_DQK:(hh + 1) * MLSTM_DQK]
        k = k_ref[:, hh * MLSTM_DQK:(hh + 1) * MLSTM_DQK]
        v = v_ref[:, hh * MLSTM_DV:(hh + 1) * MLSTM_DV]
        i_col = a[:, hh:hh + 1]
        i_row = at[hh:hh + 1, :]
        g_col = g_cols[:, MLSTM_HEADS + hh:MLSTM_HEADS + hh + 1]
        g_row = g_rows[MLSTM_HEADS + hh:MLSTM_HEADS + hh + 1, :]
        m_prev = m_ref[hh][:, 0:1]
        c_prev = c_ref[hh]
        n_prev = n_ref[hh]

        m_inter = g_col + m_prev
        dmat = jnp.where(causal, g_col - g_row + i_row, NEG_INF)
        m_t = jnp.maximum(m_inter, jnp.max(dmat, axis=-1, keepdims=True))
        qk = lax.dot_general(q, k, (((1,), (1,)), ((), ())), preferred_element_type=F32)
        scores = qk * jnp.exp(dmat - m_t)
        inter = jnp.exp(m_inter - m_t)
        q_c = jnp.dot(q, c_prev.astype(BF16), preferred_element_type=F32)
        num = jnp.dot(scores.astype(BF16), v, preferred_element_type=F32) + inter * q_c
        q_n = jnp.sum(q.astype(F32) * n_prev, axis=-1, keepdims=True)
        den = jnp.sum(scores, axis=-1, keepdims=True) + inter * q_n
        h_out = num / jnp.maximum(jnp.abs(den), jnp.exp(-m_t))

        g_last = g_col[L - 1:L, :]
        log_w = g_last - g_col + i_col
        m_new = jnp.maximum(g_last + m_prev, jnp.max(log_w, axis=0, keepdims=True))
        w_col = jnp.exp(log_w - m_new)
        decay = jnp.exp(g_last + m_prev - m_new)
        kw = k.astype(F32) * w_col
        upd = jnp.dot(kw.T.astype(BF16), v, preferred_element_type=F32)
        c_ref[hh] = decay * c_prev + upd
        n_ref[hh] = decay * n_prev + jnp.sum(kw, axis=0, keepdims=True)
        m_ref[hh] = jnp.broadcast_to(m_new, (1, LANES))

        hn = h_out * lax.rsqrt(jnp.mean(h_out * h_out, axis=-1, keepdims=True) + EPS)
        hn = hn * gain_ref[:, hh * MLSTM_DV:(hh + 1) * MLSTM_DV]
        mo = mo_ref[:, hh * MLSTM_DV:(hh + 1) * MLSTM_DV].astype(F32)
        o_ref[:, hh * MLSTM_DV:(hh + 1) * MLSTM_DV] = (hn * _sigmoid(mo)).astype(BF16)


def _mlstm(p, a, at, gain_row):
    s = p.shape[0]
    L = MLSTM_L
    qw = MLSTM_HEADS * MLSTM_DQK
    return pl.pallas_call(
        _mlstm_kernel,
        grid=(s // L,),
        in_specs=[pl.BlockSpec((L, qw), lambda i: (i, 0)),
                  pl.BlockSpec((L, qw), lambda i: (i, 1)),
                  pl.BlockSpec((L, MLSTM_WIDTH), lambda i: (i, 1)),
                  pl.BlockSpec((L, MLSTM_WIDTH), lambda i: (i, 2)),
                  pl.BlockSpec((L, LANES), lambda i: (i, 0)),
                  pl.BlockSpec((16, L), lambda i: (0, i)),
                  pl.BlockSpec((1, MLSTM_WIDTH), lambda i: (0, 0))],
        out_specs=pl.BlockSpec((L, MLSTM_WIDTH), lambda i: (i, 0)),
        out_shape=jax.ShapeDtypeStruct((s, MLSTM_WIDTH), BF16),
        scratch_shapes=[pltpu.VMEM((MLSTM_HEADS, MLSTM_DQK, MLSTM_DV), F32),
                        pltpu.VMEM((MLSTM_HEADS, 1, MLSTM_DQK), F32),
                        pltpu.VMEM((MLSTM_HEADS, 1, LANES), F32)],
        compiler_params=_params(1, 32),
        name="mlstm",
    )(p, p, p, p, a, at, gain_row)


def _fox_kernel(q_ref, k_ref, v_ref, frow_ref, fref_ref, gain_ref, o_ref):
    qi = pl.program_id(1)
    bq = FOX_BQ
    q = q_ref[...]
    fref = fref_ref[0, 0]

    def step(ki, carry, masked):
        m, l, acc = carry
        off = pl.multiple_of(ki * bq, bq)
        k = k_ref[pl.ds(off, bq), :]
        v = v_ref[pl.ds(off, bq), :]
        s = lax.dot_general(q, k, (((1,), (1,)), ((), ())), preferred_element_type=F32)
        s = s + (fref - frow_ref[0, :, pl.ds(off, bq)])
        if masked:
            r = lax.broadcasted_iota(I32, (bq, bq), 0)
            c = lax.broadcasted_iota(I32, (bq, bq), 1)
            s = jnp.where(c <= r, s, NEG_INF)
        m_new = jnp.maximum(m, jnp.max(s, axis=-1, keepdims=True))
        alpha = jnp.exp(m - m_new)
        p = jnp.exp(s - m_new)
        l = alpha * l + jnp.sum(p, axis=-1, keepdims=True)
        acc = alpha * acc + jnp.dot(p.astype(BF16), v, preferred_element_type=F32)
        return m_new, l, acc

    init = (jnp.full((bq, 1), NEG_INF, F32), jnp.zeros((bq, 1), F32), jnp.zeros((bq, FOX_HEAD_DIM), F32))
    carry = lax.fori_loop(0, qi, lambda ki, c: step(ki, c, False), init)
    _, l, acc = step(qi, carry, True)
    out = acc / l
    out = out * lax.rsqrt(jnp.mean(out * out, axis=-1, keepdims=True) + EPS) * gain_ref[0]
    o_ref[...] = out.astype(BF16)


def _fox_attn(p, frow3, fref4, gain3):
    s = p.shape[0]
    bq = FOX_BQ
    qcol = 3072 // FOX_HEAD_DIM
    kcol = 4096 // FOX_HEAD_DIM
    vcol = 5120 // FOX_HEAD_DIM
    return pl.pallas_call(
        _fox_kernel,
        grid=(FOX_HEADS, s // bq),
        in_specs=[pl.BlockSpec((bq, FOX_HEAD_DIM), lambda h, i: (i, qcol + h)),
                  pl.BlockSpec((s, FOX_HEAD_DIM), lambda h, i: (0, kcol + h)),
                  pl.BlockSpec((s, FOX_HEAD_DIM), lambda h, i: (0, vcol + h)),
                  pl.BlockSpec((1, 1, s), lambda h, i: (h, 0, 0)),
                  pl.BlockSpec((1, 1, 1, bq), lambda h, i: (h, i, 0, 0)),
                  pl.BlockSpec((1, 1, FOX_HEAD_DIM), lambda h, i: (h, 0, 0))],
        out_specs=pl.BlockSpec((bq, FOX_HEAD_DIM), lambda h, i: (i, h)),
        out_shape=jax.ShapeDtypeStruct((s, FOX_WIDTH), BF16),
        compiler_params=_params(2, 40),
        name="fox_attn",
    )(p, p, p, frow3, fref4, gain3)


def _outrouter_kernel(hm_ref, hf_ref, wt_ref, wb_ref, x_ref, g1_ref, nf_ref, sh2_ref, sc2_ref, wr_ref, br_ref,
                      hres_ref, h2p_ref, route_ref, cnt_ref, carry_ref):
    i = pl.program_id(0)
    tm = x_ref.shape[0]

    @pl.when(i == 0)
    def _():
        carry_ref[...] = jnp.zeros_like(carry_ref)

    y = (jnp.dot(hm_ref[...], wt_ref[...], preferred_element_type=F32)
         + jnp.dot(hf_ref[...], wb_ref[...], preferred_element_type=F32))
    hres = x_ref[...] + g1_ref[...] * y
    hres_ref[...] = hres
    ms = jnp.mean(hres * hres, axis=-1, keepdims=True)
    h2 = hres * lax.rsqrt(ms + EPS) * nf_ref[...] * (1.0 + sc2_ref[...]) + sh2_ref[...]

    bits = lax.bitcast_convert_type(h2.astype(BF16).astype(F32), U32)
    h2p_ref[...] = (bits[:, HALF:] & jnp.uint32(0xFFFF0000)) | (bits[:, :HALF] >> 16)

    logits = jnp.dot(h2, wr_ref[...], precision=HIGHEST, preferred_element_type=F32) + br_ref[...]
    col = lax.broadcasted_iota(I32, logits.shape, 1)
    colf = col.astype(F32)
    lg = jnp.where(col < N_EXPERTS, logits, NEG_INF)
    vals, idxs = [], []
    for _ in range(TOP_K):
        mx = jnp.max(lg, axis=-1, keepdims=True)
        idx = jnp.min(jnp.where(lg == mx, colf, float(LANES)), axis=-1, keepdims=True)
        vals.append(mx)
        idxs.append(idx)
        lg = jnp.where(colf == idx, NEG_INF, lg)
    exps = [jnp.exp(vv - vals[0]) for vv in vals]
    denom = exps[0] + exps[1] + exps[2] + exps[3]
    sel = jnp.zeros(logits.shape, F32)
    for idx in idxs:
        sel = sel + (colf == idx).astype(F32)

    r = lax.broadcasted_iota(I32, (tm, tm), 0)
    cc = lax.broadcasted_iota(I32, (tm, tm), 1)
    tril = (cc <= r).astype(BF16)
    incl = jnp.dot(tril, sel.astype(BF16), preferred_element_type=F32) + carry_ref[...]
    carry_ref[...] = incl[tm - 1:tm, :]
    cnt_ref[...] = incl[tm - 1:tm, :]
    excl = incl - sel

    route = jnp.zeros(logits.shape, F32)
    for kk in range(TOP_K):
        rank = jnp.sum(jnp.where(colf == idxs[kk], excl, 0.0), axis=-1, keepdims=True)
        route = route + jnp.where(col == kk, idxs[kk], 0.0)
        route = route + jnp.where(col == TOP_K + kk, rank, 0.0)
        route = route + jnp.where(col == 2 * TOP_K + kk, exps[kk] / denom, 0.0)
    route_ref[...] = route


def _out_router(hm, hf, wt, wb, x2, g1, nf, sh2, sc2, wr, br):
    s, d = x2.shape
    tm = OUT_TM
    row = lambda i: (0, 0)
    return pl.pallas_call(
        _outrouter_kernel,
        grid=(s // tm,),
        in_specs=[pl.BlockSpec((tm, MLSTM_WIDTH), lambda i: (i, 0)),
                  pl.BlockSpec((tm, FOX_WIDTH), lambda i: (i, 0)),
                  pl.BlockSpec((MLSTM_WIDTH, d), row),
                  pl.BlockSpec((FOX_WIDTH, d), row),
                  pl.BlockSpec((tm, d), lambda i: (i, 0)),
                  pl.BlockSpec((1, d), row), pl.BlockSpec((1, d), row),
                  pl.BlockSpec((1, d), row), pl.BlockSpec((1, d), row),
                  pl.BlockSpec((d, LANES), row), pl.BlockSpec((1, LANES), row)],
        out_specs=[pl.BlockSpec((tm, d), lambda i: (i, 0)),
                   pl.BlockSpec((tm, HALF), lambda i: (i, 0)),
                   pl.BlockSpec((tm, LANES), lambda i: (i, 0)),
                   pl.BlockSpec((1, LANES), row)],
        out_shape=[jax.ShapeDtypeStruct((s, d), F32),
                   jax.ShapeDtypeStruct((s, HALF), U32),
                   jax.ShapeDtypeStruct((s, LANES), F32),
                   jax.ShapeDtypeStruct((1, LANES), F32)],
        scratch_shapes=[pltpu.VMEM((1, LANES), F32)],
        compiler_params=_params(1, 48),
        name="out_router",
    )(hm, hf, wt, wb, x2, g1, nf, sh2, sc2, wr, br)


def _row_copy(src_hbm, src_row, dst_hbm, dst_row, sem):
    return pltpu.make_async_copy(src_hbm.at[pl.ds(src_row, 1)], dst_hbm.at[pl.ds(dst_row, 1)], sem)


def _dispatch_kernel(zstart_ref, zflag_ref, dest_ref, h2p_hbm, xs_hbm, zbuf, zsem, sem):
    i = pl.program_id(0)
    n = PAIRS_PER_STEP

    @pl.when(i == 0)
    def _():
        zbuf[...] = jnp.zeros_like(zbuf)

        def zero_copy(e):
            z0 = pl.multiple_of(zstart_ref[e], SUB_ROWS)
            return pltpu.make_async_copy(zbuf, xs_hbm.at[pl.ds(z0, SUB_ROWS)], zsem)

        for e in range(N_EXPERTS):
            @pl.when(zflag_ref[e] == 1)
            def _():
                zero_copy(e).start()
        for e in range(N_EXPERTS):
            @pl.when(zflag_ref[e] == 1)
            def _():
                zero_copy(e).wait()

    def issue(j, _):
        tok = lax.shift_right_logical(i * n + j, 2)
        _row_copy(h2p_hbm, tok, xs_hbm, dest_ref[0, 0, j], sem).start()
        return 0

    lax.fori_loop(0, n, issue, 0)

    def drain(j, _):
        _row_copy(h2p_hbm, 0, xs_hbm, 0, sem).wait()
        return 0

    lax.fori_loop(0, n, drain, 0)


def _dispatch(zstart, zflag, dest3, h2p, n_rows):
    steps = dest3.shape[0]
    grid_spec = pltpu.PrefetchScalarGridSpec(
        num_scalar_prefetch=2,
        grid=(steps,),
        in_specs=[pl.BlockSpec((1, 1, PAIRS_PER_STEP), lambda i, zs, zf: (i, 0, 0), memory_space=pltpu.SMEM),
                  pl.BlockSpec(memory_space=pl.ANY)],
        out_specs=pl.BlockSpec(memory_space=pl.ANY),
        scratch_shapes=[pltpu.VMEM((SUB_ROWS, HALF), U32),
                        pltpu.SemaphoreType.DMA(()),
                        pltpu.SemaphoreType.DMA(())],
    )
    return pl.pallas_call(
        _dispatch_kernel,
        grid_spec=grid_spec,
        out_shape=jax.ShapeDtypeStruct((n_rows, HALF), U32),
        compiler_params=_params(1, 16),
        name="dispatch",
    )(zstart, zflag, dest3, h2p)


def _expert_kernel(ce_ref, cn_ref, cmap_ref, nact_ref,
                   x_ref, wg_ref, wu_ref, wd_ref, bg_ref, bu_ref, bd_ref, o_ref,
                   wgb_ref, wub_ref, wdb_ref):
    c = pl.program_id(0)
    f = pl.program_id(1)

    @pl.when(c < nact_ref[0])
    def _():
        wgb_ref[...] = wg_ref[...].astype(BF16)
        wub_ref[...] = wu_ref[...].astype(BF16)
        wdb_ref[...] = wd_ref[...].astype(BF16)
        bg = bg_ref[...]
        bu = bu_ref[...]
        bd = bd_ref[...]
        n_sub = lax.shift_right_logical(cn_ref[c] + (SUB_ROWS - 1), SUB_ROWS.bit_length() - 1)

        def body(sb, _):
            r0 = pl.multiple_of(sb * SUB_ROWS, SUB_ROWS)
            xu = x_ref[pl.ds(r0, SUB_ROWS), :]
            xl = lax.bitcast_convert_type(xu << 16, F32).astype(BF16)
            xh = lax.bitcast_convert_type(xu & jnp.uint32(0xFFFF0000), F32).astype(BF16)
            gt = (jnp.dot(xl, wgb_ref[:HALF, :], preferred_element_type=F32)
                  + jnp.dot(xh, wgb_ref[HALF:, :], preferred_element_type=F32) + bg)
            up = (jnp.dot(xl, wub_ref[:HALF, :], preferred_element_type=F32)
                  + jnp.dot(xh, wub_ref[HALF:, :], preferred_element_type=F32) + bu)
            gate = jnp.minimum(gt, SWIGLU_LIMIT)
            up = jnp.clip(up, -SWIGLU_LIMIT, SWIGLU_LIMIT)
            act = (up + 1.0) * gate * _sigmoid(SWIGLU_ALPHA * gate)
            y = jnp.dot(act.astype(BF16), wdb_ref[...], preferred_element_type=F32)

            @pl.when(f == 0)
            def _():
                o_ref[pl.ds(r0, SUB_ROWS), :] = y + bd

            @pl.when(f != 0)
            def _():
                o_ref[pl.ds(r0, SUB_ROWS), :] += y

            return 0

        lax.fori_loop(0, n_sub, body, 0)


def _experts(ce, cn, cmap, nact, xs, w_up_gate, w_down, b_up_gate3, b_down3, n_chunks):
    rc, tf = ROWS_PER_CHUNK, FF_TILE
    nf = D_FF // tf
    d = D_MODEL

    def fsel(c, f, nact_ref):
        return jnp.where(c < nact_ref[0], f, nf - 1)

    grid_spec = pltpu.PrefetchScalarGridSpec(
        num_scalar_prefetch=4,
        grid=(n_chunks, nf),
        in_specs=[
            pl.BlockSpec((rc, HALF), lambda c, f, ce, cn, cm, na: (cm[c], 0)),
            pl.BlockSpec((None, d, tf), lambda c, f, ce, cn, cm, na: (ce[c], 0, fsel(c, f, na))),
            pl.BlockSpec((None, d, tf), lambda c, f, ce, cn, cm, na: (ce[c], 0, nf + fsel(c, f, na))),
            pl.BlockSpec((None, tf, d), lambda c, f, ce, cn, cm, na: (ce[c], fsel(c, f, na), 0)),
            pl.BlockSpec((None, 1, tf), lambda c, f, ce, cn, cm, na: (ce[c], 0, fsel(c, f, na))),
            pl.BlockSpec((None, 1, tf), lambda c, f, ce, cn, cm, na: (ce[c], 0, nf + fsel(c, f, na))),
            pl.BlockSpec((None, 1, d), lambda c, f, ce, cn, cm, na: (ce[c], 0, 0)),
        ],
        out_specs=pl.BlockSpec((rc, d), lambda c, f, ce, cn, cm, na: (cm[c], 0)),
        scratch_shapes=[pltpu.VMEM((d, tf), BF16), pltpu.VMEM((d, tf), BF16), pltpu.VMEM((tf, d), BF16)],
    )
    return pl.pallas_call(
        _expert_kernel,
        grid_spec=grid_spec,
        out_shape=jax.ShapeDtypeStruct((n_chunks * rc, d), F32),
        compiler_params=_params(2, 58),
        name="experts",
    )(ce, cn, cmap, nact, xs, w_up_gate, w_up_gate, w_down, b_up_gate3, b_up_gate3, b_down3)


def _combine_rows_kernel(dest_ref, ys_hbm, yg_hbm, sem, *, n_tokens):
    i = pl.program_id(0)
    n = PAIRS_PER_STEP

    def issue(j, _):
        pair = i * n + j
        tok = lax.shift_right_logical(pair, 2)
        slot = pair & (TOP_K - 1)
        _row_copy(ys_hbm, dest_ref[0, 0, j], yg_hbm, slot * n_tokens + tok, sem).start()
        return 0

    lax.fori_loop(0, n, issue, 0)

    def drain(j, _):
        _row_copy(ys_hbm, 0, yg_hbm, 0, sem).wait()
        return 0

    lax.fori_loop(0, n, drain, 0)


def _combine_rows(dest3, ys, n_tokens):
    steps = dest3.shape[0]
    d = ys.shape[1]
    return pl.pallas_call(
        functools.partial(_combine_rows_kernel, n_tokens=n_tokens),
        grid=(steps,),
        in_specs=[pl.BlockSpec((1, 1, PAIRS_PER_STEP), lambda i: (i, 0, 0), memory_space=pltpu.SMEM),
                  pl.BlockSpec(memory_space=pl.ANY)],
        out_specs=pl.BlockSpec(memory_space=pl.ANY),
        out_shape=jax.ShapeDtypeStruct((TOP_K * n_tokens, d), F32),
        scratch_shapes=[pltpu.SemaphoreType.DMA(())],
        compiler_params=_params(1, 16),
        name="combine_rows",
    )(dest3, ys)


def _final_kernel(yg_ref, hres_ref, route_ref, g2_ref, nf_ref, shf_ref, scf_ref, o_ref):
    route = route_ref[...]
    ffn = yg_ref[0] * route[:, 2 * TOP_K:2 * TOP_K + 1]
    for kk in range(1, TOP_K):
        ffn = ffn + yg_ref[kk] * route[:, 2 * TOP_K + kk:2 * TOP_K + kk + 1]
    h = hres_ref[...] + g2_ref[...] * ffn
    ms = jnp.mean(h * h, axis=-1, keepdims=True)
    o_ref[...] = h * lax.rsqrt(ms + EPS) * nf_ref[...] * (1.0 + scf_ref[...]) + shf_ref[...]


def _final(yg3, hres, route, g2, nf, shf, scf):
    s, d = hres.shape
    tm = FINAL_TM
    row = lambda i: (0, 0)
    return pl.pallas_call(
        _final_kernel,
        grid=(s // tm,),
        in_specs=[pl.BlockSpec((TOP_K, tm, d), lambda i: (0, i, 0)),
                  pl.BlockSpec((tm, d), lambda i: (i, 0)),
                  pl.BlockSpec((tm, LANES), lambda i: (i, 0)),
                  pl.BlockSpec((1, d), row), pl.BlockSpec((1, d), row),
                  pl.BlockSpec((1, d), row), pl.BlockSpec((1, d), row)],
        out_specs=pl.BlockSpec((tm, d), lambda i: (i, 0)),
        out_shape=jax.ShapeDtypeStruct((s, d), F32),
        compiler_params=_params(1, 40),
        name="final",
    )(yg3, hres, route, g2, nf, shf, scf)


def kernel(x, c, w_ada, b_ada, norm_mix, w_in, b_i, b_f, fox_b_f, fox_q_norm, fox_k_norm, mlstm_out_norm,
           fox_out_norm, w_out, norm_ffn, w_router, b_router, w_up_gate, b_up_gate, w_down, b_down,
           w_ada_final, b_ada_final, norm_final):
    b, s, d = x.shape
    assert b == 1 and d == D_MODEL and w_ada.shape[0] == 1
    x2 = x.reshape(s, d).astype(F32)
    c_col = c.astype(F32).reshape(d, 1)

    mod = _ada_mod(c_col, w_ada[0], b_ada[0].reshape(1, -1))
    sh1, sc1, g1, sh2, sc2, g2 = [mod[:, i * d:(i + 1) * d] for i in range(6)]
    modf = _ada_mod(c_col, w_ada_final, b_ada_final.reshape(1, -1))
    shf, scf = modf[:, :d], modf[:, d:]

    w = w_in[0]
    o_gate = 2 * MLSTM_HEADS * MLSTM_DQK + 2 * MLSTM_WIDTH
    o_fq = o_gate + 2 * MLSTM_HEADS
    o_ff = o_fq + 3 * FOX_WIDTH
    wp = jnp.concatenate([w[:, :o_gate], w[:, o_fq:o_ff]], axis=1).astype(BF16)
    wg = jnp.concatenate([w[:, o_gate:o_fq], w[:, o_ff:o_ff + FOX_HEADS],
                          jnp.zeros((d, LANES - 16), F32)], axis=1).astype(BF16)
    qscale = MLSTM_DQK ** -0.5
    fscale = FOX_HEAD_DIM ** -0.5
    ones = lambda n: jnp.ones((n,), F32)
    colgain = jnp.concatenate([
        ones(MLSTM_HEADS * MLSTM_DQK) * qscale, ones(MLSTM_HEADS * MLSTM_DQK), ones(2 * MLSTM_WIDTH),
        jnp.tile(fox_q_norm[0].astype(F32), FOX_HEADS) * fscale, jnp.tile(fox_k_norm[0].astype(F32), FOX_HEADS),
        ones(FOX_WIDTH)]).reshape(1, P_WIDTH)

    p, gates = _in_proj(x2, norm_mix[0].reshape(1, d), sh1, sc1, wp, wg, colgain)

    gate_bias = jnp.concatenate([b_i[0], b_f[0], fox_b_f[0], jnp.zeros((LANES - 16,), F32)]).reshape(1, LANES)
    a = _gates(gates, gate_bias)
    at = a[:, :16].T

    hm = _mlstm(p, a, at, mlstm_out_norm[0].reshape(1, MLSTM_WIDTH).astype(F32))

    nq = s // FOX_BQ
    frow = at[2 * MLSTM_HEADS:2 * MLSTM_HEADS + FOX_HEADS]
    fref4 = jnp.broadcast_to(frow[:, ::FOX_BQ][:, :, None, None], (FOX_HEADS, nq, 1, FOX_BQ))
    hf = _fox_attn(p, frow.reshape(FOX_HEADS, 1, s), fref4,
                   fox_out_norm[0].reshape(FOX_HEADS, 1, FOX_HEAD_DIM).astype(F32))

    wo = w_out[0].astype(BF16)
    wr = jnp.concatenate([w_router[0].astype(F32), jnp.zeros((d, LANES - N_EXPERTS), F32)], axis=1)
    br = jnp.concatenate([b_router[0].astype(F32), jnp.zeros((LANES - N_EXPERTS,), F32)]).reshape(1, LANES)
    hres, h2p, route, cnt = _out_router(hm, hf, wo[:MLSTM_WIDTH], wo[MLSTM_WIDTH:], x2, g1,
                                        norm_ffn[0].reshape(1, d), sh2, sc2, wr, br)

    rc = ROWS_PER_CHUNK
    n_chunks = N_EXPERTS + (s * TOP_K) // rc
    counts = cnt[0, :N_EXPERTS].astype(I32)
    eidx = route[:, :TOP_K].astype(I32)
    rank = route[:, TOP_K:2 * TOP_K].astype(I32)
    nch = (counts + rc - 1) // rc
    cend = jnp.cumsum(nch)
    cstart = cend - nch
    nact = cend[-1]
    dest = (cstart[eidx] * rc + rank).reshape(-1, 1, PAIRS_PER_STEP)
    cidx = jnp.arange(n_chunks, dtype=I32)
    cmap = jnp.minimum(cidx, nact - 1)
    ce = jnp.minimum(jnp.searchsorted(cend, cmap, side="right"), N_EXPERTS - 1).astype(I32)
    cn = jnp.where(cidx < nact, jnp.clip(counts[ce] - (cmap - cstart[ce]) * rc, 0, rc), 0).astype(I32)
    zflag = (counts % SUB_ROWS != 0).astype(I32)
    zstart = (cstart * rc + (counts // SUB_ROWS) * SUB_ROWS).astype(I32)

    xs = _dispatch(zstart, zflag, dest, h2p, n_chunks * rc)
    ys = _experts(ce, cn, cmap, nact.reshape(1).astype(I32), xs, w_up_gate[0], w_down[0],
                  b_up_gate[0].reshape(N_EXPERTS, 1, 2 * D_FF), b_down[0].reshape(N_EXPERTS, 1, d), n_chunks)
    yg = _combine_rows(dest, ys, s)
    out = _final(yg.reshape(TOP_K, s, d), hres, route, g2, norm_final.reshape(1, d), shf, scf)
    return out.reshape(b, s, d).astype(x.dtype)
```

```python
import jax
import jax.numpy as jnp
from jax import lax
from jax.experimental import pallas as pl
from jax.experimental.pallas import tpu as pltpu

F32 = jnp.float32
BF16 = jnp.bfloat16
I32 = jnp.int32
NEG_INF = float("-inf")
HIGHEST = lax.Precision.HIGHEST

D_MODEL = 2048
MLSTM_HEADS = 4
MLSTM_DQK = 128
MLSTM_DV = 256
MLSTM_WIDTH = MLSTM_HEADS * MLSTM_DV
GATE_SOFTCAP = 15.0
FOX_HEADS = 8
FOX_HEAD_DIM = 128
FOX_WIDTH = FOX_HEADS * FOX_HEAD_DIM
N_EXPERTS = 32
TOP_K = 4
D_FF = 2048
SWIGLU_LIMIT = 7.0
SWIGLU_ALPHA = 1.702
EPS = 1e-6

LANES = 128
P_WIDTH = 6144

ADA_TN = 512
INPROJ_TM = 1024
INPROJ_TN = 512
GATES_TL = 512
MLSTM_L = 256
FOX_BQ = 512
OUT_TM = 256
ROWS_PER_CHUNK = 1280
SUB_ROWS = 256
FF_TILE = 256
PAIRS_PER_STEP = 2048
FINAL_TM = 256
MIB = 1024 * 1024


def _params(n_axes, vmem_mib):
    return pltpu.CompilerParams(dimension_semantics=("arbitrary",) * n_axes,
                                vmem_limit_bytes=vmem_mib * MIB)


def _log_sigmoid(z):
    return jnp.minimum(z, 0.0) - jnp.log1p(jnp.exp(-jnp.abs(z)))


def _sigmoid(z):
    return 1.0 / (1.0 + jnp.exp(-z))


def _ada_kernel(c_ref, w_ref, b_ref, o_ref):
    c = c_ref[...]
    ca = c * _sigmoid(c)
    o_ref[...] = jnp.sum(ca * w_ref[...], axis=0, keepdims=True) + b_ref[...]


def _ada_mod(c_col, w, b_row):
    d, n = w.shape
    return pl.pallas_call(
        _ada_kernel,
        grid=(n // ADA_TN,),
        in_specs=[pl.BlockSpec((d, 1), lambda j: (0, 0)),
                  pl.BlockSpec((d, ADA_TN), lambda j: (0, j)),
                  pl.BlockSpec((1, ADA_TN), lambda j: (0, j))],
        out_specs=pl.BlockSpec((1, ADA_TN), lambda j: (0, j)),
        out_shape=jax.ShapeDtypeStruct((1, n), F32),
        compiler_params=_params(1, 32),
        name="ada_mod",
    )(c_col, w, b_row)


_FQ_COL = 2 * MLSTM_HEADS * MLSTM_DQK + 2 * MLSTM_WIDTH
_NORM_TILE_LO = _FQ_COL // INPROJ_TN
_NORM_TILE_HI = (_FQ_COL + 2 * FOX_WIDTH) // INPROJ_TN


def _inproj_kernel(x_ref, g_ref, sh_ref, sc_ref, w_ref, wg_ref, cg_ref, p_ref, gate_ref, hn_ref):
    j = pl.program_id(1)

    @pl.when(j == 0)
    def _():
        x = x_ref[...]
        ms = jnp.mean(x * x, axis=-1, keepdims=True)
        hn = x * lax.rsqrt(ms + EPS) * g_ref[...] * (1.0 + sc_ref[...]) + sh_ref[...]
        hnb = hn.astype(BF16)
        hn_ref[...] = hnb
        gate_ref[...] = jnp.dot(hnb, wg_ref[...], preferred_element_type=F32)

    acc = jnp.dot(hn_ref[...], w_ref[...], preferred_element_type=F32)
    is_norm = jnp.logical_and(j >= _NORM_TILE_LO, j < _NORM_TILE_HI)

    @pl.when(is_norm)
    def _():
        parts = []
        for hh in range(INPROJ_TN // FOX_HEAD_DIM):
            a = acc[:, hh * FOX_HEAD_DIM:(hh + 1) * FOX_HEAD_DIM]
            parts.append(a * lax.rsqrt(jnp.mean(a * a, axis=-1, keepdims=True) + EPS))
        p_ref[...] = (jnp.concatenate(parts, axis=-1) * cg_ref[...]).astype(BF16)

    @pl.when(jnp.logical_not(is_norm))
    def _():
        p_ref[...] = (acc * cg_ref[...]).astype(BF16)


def _in_proj(x2, g_row, sh_row, sc_row, wp, wg, colgain):
    s, d = x2.shape
    tm, tn = INPROJ_TM, INPROJ_TN
    return pl.pallas_call(
        _inproj_kernel,
        grid=(s // tm, P_WIDTH // tn),
        in_specs=[pl.BlockSpec((tm, d), lambda i, j: (i, 0)),
                  pl.BlockSpec((1, d), lambda i, j: (0, 0)),
                  pl.BlockSpec((1, d), lambda i, j: (0, 0)),
                  pl.BlockSpec((1, d), lambda i, j: (0, 0)),
                  pl.BlockSpec((d, tn), lambda i, j: (0, j)),
                  pl.BlockSpec((d, LANES), lambda i, j: (0, 0)),
                  pl.BlockSpec((1, tn), lambda i, j: (0, j))],
        out_specs=[pl.BlockSpec((tm, tn), lambda i, j: (i, j)),
                   pl.BlockSpec((tm, LANES), lambda i, j: (i, 0))],
        out_shape=[jax.ShapeDtypeStruct((s, P_WIDTH), BF16),
                   jax.ShapeDtypeStruct((s, LANES), F32)],
        scratch_shapes=[pltpu.VMEM((tm, d), BF16)],
        compiler_params=_params(2, 48),
        name="in_proj",
    )(x2, g_row, sh_row, sc_row, wp, wg, colgain)


def _gates_kernel(g_ref, b_ref, a_ref, carry_ref):
    i = pl.program_id(0)

    @pl.when(i == 0)
    def _():
        carry_ref[...] = jnp.zeros_like(carry_ref)

    tl = g_ref.shape[0]
    z = g_ref[...] + b_ref[...]
    col = lax.broadcasted_iota(I32, z.shape, 1)
    capped = GATE_SOFTCAP * jnp.tanh(z / GATE_SOFTCAP)
    r = lax.broadcasted_iota(I32, (tl, tl), 0)
    cc = lax.broadcasted_iota(I32, (tl, tl), 1)
    tril = (cc <= r).astype(F32)
    cs = jnp.dot(tril, _log_sigmoid(z), precision=HIGHEST, preferred_element_type=F32) + carry_ref[...]
    carry_ref[...] = cs[tl - 1:tl, :]
    a_ref[...] = jnp.where(col < MLSTM_HEADS, capped,
                           jnp.where(col < 2 * MLSTM_HEADS, _log_sigmoid(capped), cs))


def _gates(gates, bias_row):
    s = gates.shape[0]
    tl = GATES_TL
    return pl.pallas_call(
        _gates_kernel,
        grid=(s // tl,),
        in_specs=[pl.BlockSpec((tl, LANES), lambda i: (i, 0)),
                  pl.BlockSpec((1, LANES), lambda i: (0, 0))],
        out_specs=pl.BlockSpec((tl, LANES), lambda i: (i, 0)),
        out_shape=jax.ShapeDtypeStruct((s, LANES), F32),
        scratch_shapes=[pltpu.VMEM((1, LANES), F32)],
        compiler_params=_params(1, 32),
        name="gates",
    )(gates, bias_row)


def _mlstm_kernel(q_ref, k_ref, v_ref, mo_ref, a_ref, at_ref, gain_ref, o_ref, c_ref, n_ref, m_ref):
    ci = pl.program_id(0)
    L = MLSTM_L

    @pl.when(ci == 0)
    def _():
        c_ref[...] = jnp.zeros_like(c_ref)
        n_ref[...] = jnp.zeros_like(n_ref)
        m_ref[...] = jnp.zeros_like(m_ref)

    r = lax.broadcasted_iota(I32, (L, L), 0)
    cc = lax.broadcasted_iota(I32, (L, L), 1)
    causal = cc <= r
    tril = causal.astype(F32)
    a = a_ref[...]
    at = at_ref[...]
    g_cols = jnp.dot(tril, a, precision=HIGHEST, preferred_element_type=F32)
    g_rows = jnp.dot(at, (r <= cc).astype(F32), precision=HIGHEST, preferred_element_type=F32)

    for hh in range(MLSTM_HEADS):
        q = q_ref[:, hh * MLSTM_DQK:(hh + 1) * MLSTM_DQK]
        k = k_ref[:, hh * MLSTM_DQK:(hh + 1) * MLSTM_DQK]
        v = v_ref[:, hh * MLSTM_DV:(hh + 1) * MLSTM_DV]
        i_col = a[:, hh:hh + 1]
        i_row = at[hh:hh + 1, :]
        g_col = g_cols[:, MLSTM_HEADS + hh:MLSTM_HEADS + hh + 1]
        g_row = g_rows[MLSTM_HEADS + hh:MLSTM_HEADS + hh + 1, :]
        m_prev = m_ref[hh][:, 0:1]
        c_prev = c_ref[hh]
        n_prev = n_ref[hh]

        m_inter = g_col + m_prev
        dmat = jnp.where(causal, g_col - g_row + i_row, NEG_INF)
        m_t = jnp.maximum(m_inter, jnp.max(dmat, axis=-1, keepdims=True))
        qk = lax.dot_general(q, k, (((1,), (1,)), ((), ())), preferred_element_type=F32)
        scores = qk * jnp.exp(dmat - m_t)
        inter = jnp.exp(m_inter - m_t)
        q_c = jnp.dot(q, c_prev.astype(BF16), preferred_element_type=F32)
        num = jnp.dot(scores.astype(BF16), v, preferred_element_type=F32) + inter * q_c
        q_n = jnp.sum(q.astype(F32) * n_prev, axis=-1, keepdims=True)
        den = jnp.sum(scores, axis=-1, keepdims=True) + inter * q_n
        h_out = num / jnp.maximum(jnp.abs(den), jnp.exp(-m_t))

        g_last = g_col[L - 1:L, :]
        log_w = g_last - g_col + i_col
        m_new = jnp.maximum(g_last + m_prev, jnp.max(log_w, axis=0, keepdims=True))
        w_col = jnp.exp(log_w - m_new)
        decay = jnp.exp(g_last + m_prev - m_new)
        kw = k.astype(F32) * w_col
        upd = jnp.dot(kw.T.astype(BF16), v, preferred_element_type=F32)
        c_ref[hh] = decay * c_prev + upd
        n_ref[hh] = decay * n_prev + jnp.sum(kw, axis=0, keepdims=True)
        m_ref[hh] = jnp.broadcast_to(m_new, (1, LANES))

        hn = h_out * lax.rsqrt(jnp.mean(h_out * h_out, axis=-1, keepdims=True) + EPS)
        hn = hn * gain_ref[:, hh * MLSTM_DV:(hh + 1) * MLSTM_DV]
        mo = mo_ref[:, hh * MLSTM_DV:(hh + 1) * MLSTM_DV].astype(F32)
        o_ref[:, hh * MLSTM_DV:(hh + 1) * MLSTM_DV] = (hn * _sigmoid(mo)).astype(BF16)


def _mlstm(p, a, at, gain_row):
    s = p.shape[0]
    L = MLSTM_L
    qw = MLSTM_HEADS * MLSTM_DQK
    return pl.pallas_call(
        _mlstm_kernel,
        grid=(s // L,),
        in_specs=[pl.BlockSpec((L, qw), lambda i: (i, 0)),
                  pl.BlockSpec((L, qw), lambda i: (i, 1)),
                  pl.BlockSpec((L, MLSTM_WIDTH), lambda i: (i, 1)),
                  pl.BlockSpec((L, MLSTM_WIDTH), lambda i: (i, 2)),
                  pl.BlockSpec((L, LANES), lambda i: (i, 0)),
                  pl.BlockSpec((16, L), lambda i: (0, i)),
                  pl.BlockSpec((1, MLSTM_WIDTH), lambda i: (0, 0))],
        out_specs=pl.BlockSpec((L, MLSTM_WIDTH), lambda i: (i, 0)),
        out_shape=jax.ShapeDtypeStruct((s, MLSTM_WIDTH), BF16),
        scratch_shapes=[pltpu.VMEM((MLSTM_HEADS, MLSTM_DQK, MLSTM_DV), F32),
                        pltpu.VMEM((MLSTM_HEADS, 1, MLSTM_DQK), F32),
                        pltpu.VMEM((MLSTM_HEADS, 1, LANES), F32)],
        compiler_params=_params(1, 32),
        name="mlstm",
    )(p, p, p, p, a, at, gain_row)


def _fox_kernel(q_ref, k_ref, v_ref, frow_ref, fref_ref, gain_ref, o_ref):
    qi = pl.program_id(1)
    bq = FOX_BQ
    q = q_ref[...]
    fref = fref_ref[0, 0]

    def step(ki, carry, masked):
        m, l, acc = carry
        off = pl.multiple_of(ki * bq, bq)
        k = k_ref[pl.ds(off, bq), :]
        v = v_ref[pl.ds(off, bq), :]
        s = lax.dot_general(q, k, (((1,), (1,)), ((), ())), preferred_element_type=F32)
        s = s + (fref - frow_ref[0, :, pl.ds(off, bq)])
        if masked:
            r = lax.broadcasted_iota(I32, (bq, bq), 0)
            c = lax.broadcasted_iota(I32, (bq, bq), 1)
            s = jnp.where(c <= r, s, NEG_INF)
        m_new = jnp.maximum(m, jnp.max(s, axis=-1, keepdims=True))
        alpha = jnp.exp(m - m_new)
        p = jnp.exp(s - m_new)
        l = alpha * l + jnp.sum(p, axis=-1, keepdims=True)
        acc = alpha * acc + jnp.dot(p.astype(BF16), v, preferred_element_type=F32)
        return m_new, l, acc

    init = (jnp.full((bq, 1), NEG_INF, F32), jnp.zeros((bq, 1), F32), jnp.zeros((bq, FOX_HEAD_DIM), F32))
    carry = lax.fori_loop(0, qi, lambda ki, c: step(ki, c, False), init)
    _, l, acc = step(qi, carry, True)
    out = acc / l
    out = out * lax.rsqrt(jnp.mean(out * out, axis=-1, keepdims=True) + EPS) * gain_ref[0]
    o_ref[...] = out.astype(BF16)


def _fox_attn(p, frow3, fref4, gain3):
    s = p.shape[0]
    bq = FOX_BQ
    qcol = 3072 // FOX_HEAD_DIM
    kcol = 4096 // FOX_HEAD_DIM
    vcol = 5120 // FOX_HEAD_DIM
    return pl.pallas_call(
        _fox_kernel,
        grid=(FOX_HEADS, s // bq),
        in_specs=[pl.BlockSpec((bq, FOX_HEAD_DIM), lambda h, i: (i, qcol + h)),
                  pl.BlockSpec((s, FOX_HEAD_DIM), lambda h, i: (0, kcol + h)),
                  pl.BlockSpec((s, FOX_HEAD_DIM), lambda h, i: (0, vcol + h)),
                  pl.BlockSpec((1, 1, s), lambda h, i: (h, 0, 0)),
                  pl.BlockSpec((1, 1, 1, bq), lambda h, i: (h, i, 0, 0)),
                  pl.BlockSpec((1, 1, FOX_HEAD_DIM), lambda h, i: (h, 0, 0))],
        out_specs=pl.BlockSpec((bq, FOX_HEAD_DIM), lambda h, i: (i, h)),
        out_shape=jax.ShapeDtypeStruct((s, FOX_WIDTH), BF16),
        compiler_params=_params(2, 40),
        name="fox_attn",
    )(p, p, p, frow3, fref4, gain3)


def _outrouter_kernel(hm_ref, hf_ref, wt_ref, wb_ref, x_ref, g1_ref, nf_ref, sh2_ref, sc2_ref, wr_ref, br_ref,
                      hres_ref, h2_ref, route_ref, cnt_ref, carry_ref):
    i = pl.program_id(0)
    tm = x_ref.shape[0]

    @pl.when(i == 0)
    def _():
        carry_ref[...] = jnp.zeros_like(carry_ref)

    y = (jnp.dot(hm_ref[...], wt_ref[...], preferred_element_type=F32)
         + jnp.dot(hf_ref[...], wb_ref[...], preferred_element_type=F32))
    hres = x_ref[...] + g1_ref[...] * y
    hres_ref[...] = hres
    ms = jnp.mean(hres * hres, axis=-1, keepdims=True)
    h2 = hres * lax.rsqrt(ms + EPS) * nf_ref[...] * (1.0 + sc2_ref[...]) + sh2_ref[...]

    h2_ref[...] = h2

    logits = jnp.dot(h2, wr_ref[...], precision=HIGHEST, preferred_element_type=F32) + br_ref[...]
    col = lax.broadcasted_iota(I32, logits.shape, 1)
    colf = col.astype(F32)
    lg = jnp.where(col < N_EXPERTS, logits, NEG_INF)
    vals, idxs = [], []
    for _ in range(TOP_K):
        mx = jnp.max(lg, axis=-1, keepdims=True)
        idx = jnp.min(jnp.where(lg == mx, colf, float(LANES)), axis=-1, keepdims=True)
        vals.append(mx)
        idxs.append(idx)
        lg = jnp.where(colf == idx, NEG_INF, lg)
    exps = [jnp.exp(vv - vals[0]) for vv in vals]
    denom = exps[0] + exps[1] + exps[2] + exps[3]
    sel = jnp.zeros(logits.shape, F32)
    for idx in idxs:
        sel = sel + (colf == idx).astype(F32)

    r = lax.broadcasted_iota(I32, (tm, tm), 0)
    cc = lax.broadcasted_iota(I32, (tm, tm), 1)
    tril = (cc <= r).astype(BF16)
    incl = jnp.dot(tril, sel.astype(BF16), preferred_element_type=F32) + carry_ref[...]
    carry_ref[...] = incl[tm - 1:tm, :]
    cnt_ref[...] = incl[tm - 1:tm, :]
    excl = incl - sel

    route = jnp.zeros(logits.shape, F32)
    for kk in range(TOP_K):
        rank = jnp.sum(jnp.where(colf == idxs[kk], excl, 0.0), axis=-1, keepdims=True)
        route = route + jnp.where(col == kk, idxs[kk], 0.0)
        route = route + jnp.where(col == TOP_K + kk, rank, 0.0)
        route = route + jnp.where(col == 2 * TOP_K + kk, exps[kk] / denom, 0.0)
    route_ref[...] = route


def _out_router(hm, hf, wt, wb, x2, g1, nf, sh2, sc2, wr, br):
    s, d = x2.shape
    tm = OUT_TM
    row = lambda i: (0, 0)
    return pl.pallas_call(
        _outrouter_kernel,
        grid=(s // tm,),
        in_specs=[pl.BlockSpec((tm, MLSTM_WIDTH), lambda i: (i, 0)),
                  pl.BlockSpec((tm, FOX_WIDTH), lambda i: (i, 0)),
                  pl.BlockSpec((MLSTM_WIDTH, d), row),
                  pl.BlockSpec((FOX_WIDTH, d), row),
                  pl.BlockSpec((tm, d), lambda i: (i, 0)),
                  pl.BlockSpec((1, d), row), pl.BlockSpec((1, d), row),
                  pl.BlockSpec((1, d), row), pl.BlockSpec((1, d), row),
                  pl.BlockSpec((d, LANES), row), pl.BlockSpec((1, LANES), row)],
        out_specs=[pl.BlockSpec((tm, d), lambda i: (i, 0)),
                   pl.BlockSpec((tm, d), lambda i: (i, 0)),
                   pl.BlockSpec((tm, LANES), lambda i: (i, 0)),
                   pl.BlockSpec((1, LANES), row)],
        out_shape=[jax.ShapeDtypeStruct((s, d), F32),
                   jax.ShapeDtypeStruct((s, d), F32),
                   jax.ShapeDtypeStruct((s, LANES), F32),
                   jax.ShapeDtypeStruct((1, LANES), F32)],
        scratch_shapes=[pltpu.VMEM((1, LANES), F32)],
        compiler_params=_params(1, 48),
        name="out_router",
    )(hm, hf, wt, wb, x2, g1, nf, sh2, sc2, wr, br)


def _row_copy(src_hbm, src_row, dst_hbm, dst_row, sem):
    return pltpu.make_async_copy(src_hbm.at[pl.ds(src_row, 1)], dst_hbm.at[pl.ds(dst_row, 1)], sem)


def _dispatch_kernel(zstart_ref, zflag_ref, dest_ref, h2_hbm, xs_hbm, zbuf, zsem, sem):
    i = pl.program_id(0)
    n = PAIRS_PER_STEP

    @pl.when(i == 0)
    def _():
        zbuf[...] = jnp.zeros_like(zbuf)

        def zero_copy(e):
            z0 = pl.multiple_of(zstart_ref[e], SUB_ROWS)
            return pltpu.make_async_copy(zbuf, xs_hbm.at[pl.ds(z0, SUB_ROWS)], zsem)

        for e in range(N_EXPERTS):
            @pl.when(zflag_ref[e] == 1)
            def _():
                zero_copy(e).start()
        for e in range(N_EXPERTS):
            @pl.when(zflag_ref[e] == 1)
            def _():
                zero_copy(e).wait()

    def issue(j, _):
        tok = lax.shift_right_logical(i * n + j, 2)
        _row_copy(h2_hbm, tok, xs_hbm, dest_ref[0, 0, j], sem).start()
        return 0

    lax.fori_loop(0, n, issue, 0, unroll=8)
    pltpu.make_async_copy(h2_hbm.at[pl.ds(0, n)], xs_hbm.at[pl.ds(0, n)], sem).wait()


def _dispatch(zstart, zflag, dest3, h2, n_rows):
    steps = dest3.shape[0]
    grid_spec = pltpu.PrefetchScalarGridSpec(
        num_scalar_prefetch=2,
        grid=(steps,),
        in_specs=[pl.BlockSpec((1, 1, PAIRS_PER_STEP), lambda i, zs, zf: (i, 0, 0), memory_space=pltpu.SMEM),
                  pl.BlockSpec(memory_space=pl.ANY)],
        out_specs=pl.BlockSpec(memory_space=pl.ANY),
        scratch_shapes=[pltpu.VMEM((SUB_ROWS, D_MODEL), F32),
                        pltpu.SemaphoreType.DMA(()),
                        pltpu.SemaphoreType.DMA(())],
    )
    return pl.pallas_call(
        _dispatch_kernel,
        grid_spec=grid_spec,
        out_shape=jax.ShapeDtypeStruct((n_rows, D_MODEL), F32),
        compiler_params=_params(1, 16),
        name="dispatch",
    )(zstart, zflag, dest3, h2)


def _expert_kernel(ce_ref, cn_ref, cmap_ref, nact_ref,
                   x_ref, wg_ref, wu_ref, wd_ref, bg_ref, bu_ref, bd_ref, o_ref,
                   wgb_ref, wub_ref, wdb_ref):
    c = pl.program_id(0)
    f = pl.program_id(1)

    @pl.when(c < nact_ref[0])
    def _():
        @pl.when(f == 0)
        def _():
            o_ref[...] = jnp.broadcast_to(bd_ref[...], o_ref.shape)

        wgb_ref[...] = wg_ref[...].astype(BF16)
        wub_ref[...] = wu_ref[...].astype(BF16)
        wdb_ref[...] = wd_ref[...].astype(BF16)
        bg = bg_ref[...]
        bu = bu_ref[...]

        def sub_block(r0, rows):
            xb = x_ref[pl.ds(r0, rows), :].astype(BF16)
            gt = jnp.dot(xb, wgb_ref[...], preferred_element_type=F32) + bg
            up = jnp.dot(xb, wub_ref[...], preferred_element_type=F32) + bu
            gate = jnp.minimum(gt, SWIGLU_LIMIT)
            up = jnp.clip(up, -SWIGLU_LIMIT, SWIGLU_LIMIT)
            act = (up + 1.0) * gate * _sigmoid(SWIGLU_ALPHA * gate)
            o_ref[pl.ds(r0, rows), :] += jnp.dot(act.astype(BF16), wdb_ref[...], preferred_element_type=F32)

        n = cn_ref[c]
        shift = SUB_ROWS.bit_length() - 1
        n_full = lax.shift_right_logical(n, shift)
        rem = n & (SUB_ROWS - 1)
        n_big = n_full + (rem > SUB_ROWS // 2).astype(I32)

        def pair(p, _):
            r0 = pl.multiple_of(p * (2 * SUB_ROWS), 2 * SUB_ROWS)
            sub_block(r0, SUB_ROWS)
            sub_block(r0 + SUB_ROWS, SUB_ROWS)
            return 0

        lax.fori_loop(0, lax.shift_right_logical(n_big, 1), pair, 0)

        @pl.when((n_big & 1) == 1)
        def _():
            sub_block(pl.multiple_of((n_big - 1) * SUB_ROWS, SUB_ROWS), SUB_ROWS)

        @pl.when(jnp.logical_and(rem > 0, rem <= SUB_ROWS // 2))
        def _():
            sub_block(pl.multiple_of(n_full * SUB_ROWS, SUB_ROWS), SUB_ROWS // 2)


def _experts(ce, cn, cmap, nact, xs, w_up_gate, w_down, b_up_gate3, b_down3, n_chunks):
    rc, tf = ROWS_PER_CHUNK, FF_TILE
    nf = D_FF // tf
    d = D_MODEL

    def fsel(c, f, nact_ref):
        return jnp.where(c < nact_ref[0], f, nf - 1)

    grid_spec = pltpu.PrefetchScalarGridSpec(
        num_scalar_prefetch=4,
        grid=(n_chunks, nf),
        in_specs=[
            pl.BlockSpec((rc, d), lambda c, f, ce, cn, cm, na: (cm[c], 0), pipeline_mode=pl.Buffered(1)),
            pl.BlockSpec((None, d, tf), lambda c, f, ce, cn, cm, na: (ce[c], 0, fsel(c, f, na))),
            pl.BlockSpec((None, d, tf), lambda c, f, ce, cn, cm, na: (ce[c], 0, nf + fsel(c, f, na))),
            pl.BlockSpec((None, tf, d), lambda c, f, ce, cn, cm, na: (ce[c], fsel(c, f, na), 0)),
            pl.BlockSpec((None, 1, tf), lambda c, f, ce, cn, cm, na: (ce[c], 0, fsel(c, f, na))),
            pl.BlockSpec((None, 1, tf), lambda c, f, ce, cn, cm, na: (ce[c], 0, nf + fsel(c, f, na))),
            pl.BlockSpec((None, 1, d), lambda c, f, ce, cn, cm, na: (ce[c], 0, 0)),
        ],
        out_specs=pl.BlockSpec((rc, d), lambda c, f, ce, cn, cm, na: (cm[c], 0)),
        scratch_shapes=[pltpu.VMEM((d, tf), BF16), pltpu.VMEM((d, tf), BF16), pltpu.VMEM((tf, d), BF16)],
    )
    return pl.pallas_call(
        _expert_kernel,
        grid_spec=grid_spec,
        out_shape=jax.ShapeDtypeStruct((n_chunks * rc, d), F32),
        compiler_params=_params(2, 58),
        name="experts",
    )(ce, cn, cmap, nact, xs, w_up_gate, w_up_gate, w_down, b_up_gate3, b_up_gate3, b_down3)


def _final_kernel(dcur_ref, dnext_ref, ys_hbm, hres_ref, route_ref, g2_ref, nf_ref, shf_ref, scf_ref, o_ref,
                  buf, sems):
    i = pl.program_id(0)
    tm = FINAL_TM
    slot = i & 1

    def issue(d_ref, dst_slot):
        def body(j, _):
            tok = lax.shift_right_logical(j, 2)
            k = j & (TOP_K - 1)
            pltpu.make_async_copy(ys_hbm.at[pl.ds(d_ref[0, 0, j], 1)],
                                  buf.at[dst_slot, k, pl.ds(tok, 1)], sems.at[dst_slot]).start()
            return 0
        lax.fori_loop(0, TOP_K * tm, body, 0, unroll=8)

    @pl.when(i == 0)
    def _():
        issue(dcur_ref, 0)

    @pl.when(i + 1 < pl.num_programs(0))
    def _():
        issue(dnext_ref, 1 - slot)

    for k in range(TOP_K):
        pltpu.make_async_copy(ys_hbm.at[pl.ds(0, tm)], buf.at[slot, k], sems.at[slot]).wait()

    route = route_ref[...]
    ffn = buf[slot, 0] * route[:, 2 * TOP_K:2 * TOP_K + 1]
    for k in range(1, TOP_K):
        ffn = ffn + buf[slot, k] * route[:, 2 * TOP_K + k:2 * TOP_K + k + 1]
    h = hres_ref[...] + g2_ref[...] * ffn
    ms = jnp.mean(h * h, axis=-1, keepdims=True)
    o_ref[...] = h * lax.rsqrt(ms + EPS) * nf_ref[...] * (1.0 + scf_ref[...]) + shf_ref[...]


def _final(dest3, ys, hres, route, g2, nf, shf, scf):
    s, d = hres.shape
    tm = FINAL_TM
    steps = s // tm
    row = lambda i: (0, 0)
    return pl.pallas_call(
        _final_kernel,
        grid=(steps,),
        in_specs=[pl.BlockSpec((1, 1, TOP_K * tm), lambda i: (i, 0, 0), memory_space=pltpu.SMEM),
                  pl.BlockSpec((1, 1, TOP_K * tm), lambda i: (jnp.minimum(i + 1, steps - 1), 0, 0),
                               memory_space=pltpu.SMEM),
                  pl.BlockSpec(memory_space=pl.ANY),
                  pl.BlockSpec((tm, d), lambda i: (i, 0)),
                  pl.BlockSpec((tm, LANES), lambda i: (i, 0)),
                  pl.BlockSpec((1, d), row), pl.BlockSpec((1, d), row),
                  pl.BlockSpec((1, d), row), pl.BlockSpec((1, d), row)],
        out_specs=pl.BlockSpec((tm, d), lambda i: (i, 0)),
        out_shape=jax.ShapeDtypeStruct((s, d), F32),
        scratch_shapes=[pltpu.VMEM((2, TOP_K, tm, d), F32), pltpu.SemaphoreType.DMA((2,))],
        compiler_params=_params(1, 40),
        name="final",
    )(dest3, dest3, ys, hres, route, g2, nf, shf, scf)


def kernel(x, c, w_ada, b_ada, norm_mix, w_in, b_i, b_f, fox_b_f, fox_q_norm, fox_k_norm, mlstm_out_norm,
           fox_out_norm, w_out, norm_ffn, w_router, b_router, w_up_gate, b_up_gate, w_down, b_down,
           w_ada_final, b_ada_final, norm_final):
    b, s, d = x.shape
    assert b == 1 and d == D_MODEL and w_ada.shape[0] == 1
    x2 = x.reshape(s, d).astype(F32)
    c_col = c.astype(F32).reshape(d, 1)

    mod = _ada_mod(c_col, w_ada[0], b_ada[0].reshape(1, -1))
    sh1, sc1, g1, sh2, sc2, g2 = [mod[:, i * d:(i + 1) * d] for i in range(6)]
    modf = _ada_mod(c_col, w_ada_final, b_ada_final.reshape(1, -1))
    shf, scf = modf[:, :d], modf[:, d:]

    w = w_in[0]
    o_gate = 2 * MLSTM_HEADS * MLSTM_DQK + 2 * MLSTM_WIDTH
    o_fq = o_gate + 2 * MLSTM_HEADS
    o_ff = o_fq + 3 * FOX_WIDTH
    wp = jnp.concatenate([w[:, :o_gate], w[:, o_fq:o_ff]], axis=1).astype(BF16)
    wg = jnp.concatenate([w[:, o_gate:o_fq], w[:, o_ff:o_ff + FOX_HEADS],
                          jnp.zeros((d, LANES - 16), F32)], axis=1).astype(BF16)
    qscale = MLSTM_DQK ** -0.5
    fscale = FOX_HEAD_DIM ** -0.5
    ones = lambda n: jnp.ones((n,), F32)
    colgain = jnp.concatenate([
        ones(MLSTM_HEADS * MLSTM_DQK) * qscale, ones(MLSTM_HEADS * MLSTM_DQK), ones(2 * MLSTM_WIDTH),
        jnp.tile(fox_q_norm[0].astype(F32), FOX_HEADS) * fscale, jnp.tile(fox_k_norm[0].astype(F32), FOX_HEADS),
        ones(FOX_WIDTH)]).reshape(1, P_WIDTH)

    p, gates = _in_proj(x2, norm_mix[0].reshape(1, d), sh1, sc1, wp, wg, colgain)

    gate_bias = jnp.concatenate([b_i[0], b_f[0], fox_b_f[0], jnp.zeros((LANES - 16,), F32)]).reshape(1, LANES)
    a = _gates(gates, gate_bias)
    at = a[:, :16].T

    hm = _mlstm(p, a, at, mlstm_out_norm[0].reshape(1, MLSTM_WIDTH).astype(F32))

    nq = s // FOX_BQ
    frow = at[2 * MLSTM_HEADS:2 * MLSTM_HEADS + FOX_HEADS]
    fref4 = jnp.broadcast_to(frow[:, ::FOX_BQ][:, :, None, None], (FOX_HEADS, nq, 1, FOX_BQ))
    hf = _fox_attn(p, frow.reshape(FOX_HEADS, 1, s), fref4,
                   fox_out_norm[0].reshape(FOX_HEADS, 1, FOX_HEAD_DIM).astype(F32))

    wo = w_out[0].astype(BF16)
    wr = jnp.concatenate([w_router[0].astype(F32), jnp.zeros((d, LANES - N_EXPERTS), F32)], axis=1)
    br = jnp.concatenate([b_router[0].astype(F32), jnp.zeros((LANES - N_EXPERTS,), F32)]).reshape(1, LANES)
    hres, h2, route, cnt = _out_router(hm, hf, wo[:MLSTM_WIDTH], wo[MLSTM_WIDTH:], x2, g1,
                                        norm_ffn[0].reshape(1, d), sh2, sc2, wr, br)

    rc = ROWS_PER_CHUNK
    n_chunks = N_EXPERTS + (s * TOP_K) // rc
    counts = cnt[0, :N_EXPERTS].astype(I32)
    eidx = route[:, :TOP_K].astype(I32)
    rank = route[:, TOP_K:2 * TOP_K].astype(I32)
    nch = (counts + rc - 1) // rc
    cend = jnp.cumsum(nch)
    cstart = cend - nch
    nact = cend[-1]
    dest = cstart[eidx] * rc + rank
    cidx = jnp.arange(n_chunks, dtype=I32)
    cmap = jnp.minimum(cidx, nact - 1)
    ce = jnp.minimum(jnp.sum((cmap[:, None] >= cend[None, :]).astype(I32), axis=1), N_EXPERTS - 1)
    cn = jnp.where(cidx < nact, jnp.clip(counts[ce] - (cmap - cstart[ce]) * rc, 0, rc), 0).astype(I32)
    zflag = (counts % SUB_ROWS != 0).astype(I32)
    zstart = (cstart * rc + (counts // SUB_ROWS) * SUB_ROWS).astype(I32)

    xs = _dispatch(zstart, zflag, dest.reshape(-1, 1, PAIRS_PER_STEP), h2, n_chunks * rc)
    ys = _experts(ce, cn, cmap, nact.reshape(1).astype(I32), xs, w_up_gate[0], w_down[0],
                  b_up_gate[0].reshape(N_EXPERTS, 1, 2 * D_FF), b_down[0].reshape(N_EXPERTS, 1, d), n_chunks)
    out = _final(dest.reshape(-1, 1, TOP_K * FINAL_TM), ys, hres, route, g2, norm_final.reshape(1, d), shf, scf)
    return out.reshape(b, s, d).astype(x.dtype)
```

```python
import jax
import jax.numpy as jnp
from jax import lax
from jax.experimental import pallas as pl
from jax.experimental.pallas import tpu as pltpu

F32 = jnp.float32
BF16 = jnp.bfloat16
I32 = jnp.int32
NEG_INF = float("-inf")
HIGHEST = lax.Precision.HIGHEST

D_MODEL = 2048
MLSTM_HEADS = 4
MLSTM_DQK = 128
MLSTM_DV = 256
MLSTM_WIDTH = MLSTM_HEADS * MLSTM_DV
GATE_SOFTCAP = 15.0
FOX_HEADS = 8
FOX_HEAD_DIM = 128
FOX_WIDTH = FOX_HEADS * FOX_HEAD_DIM
N_EXPERTS = 32
TOP_K = 4
D_FF = 2048
SWIGLU_LIMIT = 7.0
SWIGLU_ALPHA = 1.702
EPS = 1e-6

LANES = 128
P_WIDTH = 6144

ADA_TN = 512
INPROJ_TM = 1024
INPROJ_TN = 512
GATES_TL = 512
MLSTM_L = 256
FOX_BQ = 512
OUT_TM = 256
ROWS_PER_CHUNK = 1152
SUB_ROWS = 256
ZERO_ROWS = SUB_ROWS // 2
ROW_PIECES = 2
FF_TILE = 256
PAIRS_PER_STEP = 2048
FINAL_TM = 256
MIB = 1024 * 1024


def _params(n_axes, vmem_mib):
    return pltpu.CompilerParams(dimension_semantics=("arbitrary",) * n_axes,
                                vmem_limit_bytes=vmem_mib * MIB)


def _log_sigmoid(z):
    return jnp.minimum(z, 0.0) - jnp.log1p(jnp.exp(-jnp.abs(z)))


def _sigmoid(z):
    return 1.0 / (1.0 + jnp.exp(-z))


def _ada_kernel(c_ref, w_ref, b_ref, o_ref):
    c = c_ref[...]
    ca = c * _sigmoid(c)
    o_ref[...] = jnp.sum(ca * w_ref[...], axis=0, keepdims=True) + b_ref[...]


def _ada_mod(c_col, w, b_row):
    d, n = w.shape
    return pl.pallas_call(
        _ada_kernel,
        grid=(n // ADA_TN,),
        in_specs=[pl.BlockSpec((d, 1), lambda j: (0, 0)),
                  pl.BlockSpec((d, ADA_TN), lambda j: (0, j)),
                  pl.BlockSpec((1, ADA_TN), lambda j: (0, j))],
        out_specs=pl.BlockSpec((1, ADA_TN), lambda j: (0, j)),
        out_shape=jax.ShapeDtypeStruct((1, n), F32),
        compiler_params=_params(1, 32),
        name="ada_mod",
    )(c_col, w, b_row)


_FQ_COL = 2 * MLSTM_HEADS * MLSTM_DQK + 2 * MLSTM_WIDTH
_NORM_TILE_LO = _FQ_COL // INPROJ_TN
_NORM_TILE_HI = (_FQ_COL + 2 * FOX_WIDTH) // INPROJ_TN


def _inproj_kernel(x_ref, g_ref, sh_ref, sc_ref, w_ref, wg_ref, cg_ref, p_ref, gate_ref, hn_ref):
    j = pl.program_id(1)

    @pl.when(j == 0)
    def _():
        x = x_ref[...]
        ms = jnp.mean(x * x, axis=-1, keepdims=True)
        hn = x * lax.rsqrt(ms + EPS) * g_ref[...] * (1.0 + sc_ref[...]) + sh_ref[...]
        hnb = hn.astype(BF16)
        hn_ref[...] = hnb
        gate_ref[...] = jnp.dot(hnb, wg_ref[...], preferred_element_type=F32)

    acc = jnp.dot(hn_ref[...], w_ref[...], preferred_element_type=F32)
    is_norm = jnp.logical_and(j >= _NORM_TILE_LO, j < _NORM_TILE_HI)

    @pl.when(is_norm)
    def _():
        parts = []
        for hh in range(INPROJ_TN // FOX_HEAD_DIM):
            a = acc[:, hh * FOX_HEAD_DIM:(hh + 1) * FOX_HEAD_DIM]
            parts.append(a * lax.rsqrt(jnp.mean(a * a, axis=-1, keepdims=True) + EPS))
        p_ref[...] = (jnp.concatenate(parts, axis=-1) * cg_ref[...]).astype(BF16)

    @pl.when(jnp.logical_not(is_norm))
    def _():
        p_ref[...] = (acc * cg_ref[...]).astype(BF16)


def _in_proj(x2, g_row, sh_row, sc_row, wp, wg, colgain):
    s, d = x2.shape
    tm, tn = INPROJ_TM, INPROJ_TN
    return pl.pallas_call(
        _inproj_kernel,
        grid=(s // tm, P_WIDTH // tn),
        in_specs=[pl.BlockSpec((tm, d), lambda i, j: (i, 0)),
                  pl.BlockSpec((1, d), lambda i, j: (0, 0)),
                  pl.BlockSpec((1, d), lambda i, j: (0, 0)),
                  pl.BlockSpec((1, d), lambda i, j: (0, 0)),
                  pl.BlockSpec((d, tn), lambda i, j: (0, j)),
                  pl.BlockSpec((d, LANES), lambda i, j: (0, 0)),
                  pl.BlockSpec((1, tn), lambda i, j: (0, j))],
        out_specs=[pl.BlockSpec((tm, tn), lambda i, j: (i, j)),
                   pl.BlockSpec((tm, LANES), lambda i, j: (i, 0))],
        out_shape=[jax.ShapeDtypeStruct((s, P_WIDTH), BF16),
                   jax.ShapeDtypeStruct((s, LANES), F32)],
        scratch_shapes=[pltpu.VMEM((tm, d), BF16)],
        compiler_params=_params(2, 48),
        name="in_proj",
    )(x2, g_row, sh_row, sc_row, wp, wg, colgain)


def _gates_kernel(g_ref, b_ref, a_ref, carry_ref):
    i = pl.program_id(0)

    @pl.when(i == 0)
    def _():
        carry_ref[...] = jnp.zeros_like(carry_ref)

    tl = g_ref.shape[0]
    z = g_ref[...] + b_ref[...]
    col = lax.broadcasted_iota(I32, z.shape, 1)
    capped = GATE_SOFTCAP * jnp.tanh(z / GATE_SOFTCAP)
    r = lax.broadcasted_iota(I32, (tl, tl), 0)
    cc = lax.broadcasted_iota(I32, (tl, tl), 1)
    tril = (cc <= r).astype(F32)
    cs = jnp.dot(tril, _log_sigmoid(z), precision=HIGHEST, preferred_element_type=F32) + carry_ref[...]
    carry_ref[...] = cs[tl - 1:tl, :]
    a_ref[...] = jnp.where(col < MLSTM_HEADS, capped,
                           jnp.where(col < 2 * MLSTM_HEADS, _log_sigmoid(capped), cs))


def _gates(gates, bias_row):
    s = gates.shape[0]
    tl = GATES_TL
    return pl.pallas_call(
        _gates_kernel,
        grid=(s // tl,),
        in_specs=[pl.BlockSpec((tl, LANES), lambda i: (i, 0)),
                  pl.BlockSpec((1, LANES), lambda i: (0, 0))],
        out_specs=pl.BlockSpec((tl, LANES), lambda i: (i, 0)),
        out_shape=jax.ShapeDtypeStruct((s, LANES), F32),
        scratch_shapes=[pltpu.VMEM((1, LANES), F32)],
        compiler_params=_params(1, 32),
        name="gates",
    )(gates, bias_row)


def _mlstm_kernel(q_ref, k_ref, v_ref, mo_ref, a_ref, at_ref, gain_ref, o_ref, c_ref, n_ref, m_ref):
    ci = pl.program_id(0)
    L = MLSTM_L

    @pl.when(ci == 0)
    def _():
        c_ref[...] = jnp.zeros_like(c_ref)
        n_ref[...] = jnp.zeros_like(n_ref)
        m_ref[...] = jnp.zeros_like(m_ref)

    r = lax.broadcasted_iota(I32, (L, L), 0)
    cc = lax.broadcasted_iota(I32, (L, L), 1)
    causal = cc <= r
    tril = causal.astype(F32)
    a = a_ref[...]
    at = at_ref[...]
    g_cols = jnp.dot(tril, a, precision=HIGHEST, preferred_element_type=F32)
    g_rows = jnp.dot(at, (r <= cc).astype(F32), precision=HIGHEST, preferred_element_type=F32)

    for hh in range(MLSTM_HEADS):
        q = q_ref[:, hh * MLSTM_DQK:(hh + 1) * MLSTM_DQK]
        k = k_ref[:, hh * MLSTM_DQK:(hh + 1) * MLSTM_DQK]
        v = v_ref[:, hh * MLSTM_DV:(hh + 1) * MLSTM_DV]
        i_col = a[:, hh:hh + 1]
        i_row = at[hh:hh + 1, :]
        g_col = g_cols[:, MLSTM_HEADS + hh:MLSTM_HEADS + hh + 1]
        g_row = g_rows[MLSTM_HEADS + hh:MLSTM_HEADS + hh + 1, :]
        m_prev = m_ref[hh][:, 0:1]
        c_prev = c_ref[hh]
        n_prev = n_ref[hh]

        m_inter = g_col + m_prev
        dmat = jnp.where(causal, g_col - g_row + i_row, NEG_INF)
        m_t = jnp.maximum(m_inter, jnp.max(dmat, axis=-1, keepdims=True))
        qk = lax.dot_general(q, k, (((1,), (1,)), ((), ())), preferred_element_type=F32)
        scores = qk * jnp.exp(dmat - m_t)
        inter = jnp.exp(m_inter - m_t)
        q_c = jnp.dot(q, c_prev.astype(BF16), preferred_element_type=F32)
        num = jnp.dot(scores.astype(BF16), v, preferred_element_type=F32) + inter * q_c
        q_n = jnp.sum(q.astype(F32) * n_prev, axis=-1, keepdims=True)
        den = jnp.sum(scores, axis=-1, keepdims=True) + inter * q_n
        h_out = num / jnp.maximum(jnp.abs(den), jnp.exp(-m_t))

        g_last = g_col[L - 1:L, :]
        log_w = g_last - g_col + i_col
        m_new = jnp.maximum(g_last + m_prev, jnp.max(log_w, axis=0, keepdims=True))
        w_col = jnp.exp(log_w - m_new)
        decay = jnp.exp(g_last + m_prev - m_new)
        kw = k.astype(F32) * w_col
        upd = jnp.dot(kw.T.astype(BF16), v, preferred_element_type=F32)
        c_ref[hh] = decay * c_prev + upd
        n_ref[hh] = decay * n_prev + jnp.sum(kw, axis=0, keepdims=True)
        m_ref[hh] = jnp.broadcast_to(m_new, (1, LANES))

        hn = h_out * lax.rsqrt(jnp.mean(h_out * h_out, axis=-1, keepdims=True) + EPS)
        hn = hn * gain_ref[:, hh * MLSTM_DV:(hh + 1) * MLSTM_DV]
        mo = mo_ref[:, hh * MLSTM_DV:(hh + 1) * MLSTM_DV].astype(F32)
        o_ref[:, hh * MLSTM_DV:(hh + 1) * MLSTM_DV] = (hn * _sigmoid(mo)).astype(BF16)


def _mlstm(p, a, at, gain_row):
    s = p.shape[0]
    L = MLSTM_L
    qw = MLSTM_HEADS * MLSTM_DQK
    return pl.pallas_call(
        _mlstm_kernel,
        grid=(s // L,),
        in_specs=[pl.BlockSpec((L, qw), lambda i: (i, 0)),
                  pl.BlockSpec((L, qw), lambda i: (i, 1)),
                  pl.BlockSpec((L, MLSTM_WIDTH), lambda i: (i, 1)),
                  pl.BlockSpec((L, MLSTM_WIDTH), lambda i: (i, 2)),
                  pl.BlockSpec((L, LANES), lambda i: (i, 0)),
                  pl.BlockSpec((16, L), lambda i: (0, i)),
                  pl.BlockSpec((1, MLSTM_WIDTH), lambda i: (0, 0))],
        out_specs=pl.BlockSpec((L, MLSTM_WIDTH), lambda i: (i, 0)),
        out_shape=jax.ShapeDtypeStruct((s, MLSTM_WIDTH), BF16),
        scratch_shapes=[pltpu.VMEM((MLSTM_HEADS, MLSTM_DQK, MLSTM_DV), F32),
                        pltpu.VMEM((MLSTM_HEADS, 1, MLSTM_DQK), F32),
                        pltpu.VMEM((MLSTM_HEADS, 1, LANES), F32)],
        compiler_params=_params(1, 32),
        name="mlstm",
    )(p, p, p, p, a, at, gain_row)


def _fox_kernel(q_ref, k_ref, v_ref, frow_ref, fref_ref, gain_ref, o_ref):
    qi = pl.program_id(1)
    bq = FOX_BQ
    q = q_ref[...]
    fref = fref_ref[0, 0]

    def step(ki, carry, masked):
        m, l, acc = carry
        off = pl.multiple_of(ki * bq, bq)
        k = k_ref[pl.ds(off, bq), :]
        v = v_ref[pl.ds(off, bq), :]
        s = lax.dot_general(q, k, (((1,), (1,)), ((), ())), preferred_element_type=F32)
        s = s + (fref - frow_ref[0, :, pl.ds(off, bq)])
        if masked:
            r = lax.broadcasted_iota(I32, (bq, bq), 0)
            c = lax.broadcasted_iota(I32, (bq, bq), 1)
            s = jnp.where(c <= r, s, NEG_INF)
        m_new = jnp.maximum(m, jnp.max(s, axis=-1, keepdims=True))
        alpha = jnp.exp(m - m_new)
        p = jnp.exp(s - m_new)
        l = alpha * l + jnp.sum(p, axis=-1, keepdims=True)
        acc = alpha * acc + jnp.dot(p.astype(BF16), v, preferred_element_type=F32)
        return m_new, l, acc

    init = (jnp.full((bq, 1), NEG_INF, F32), jnp.zeros((bq, 1), F32), jnp.zeros((bq, FOX_HEAD_DIM), F32))
    carry = lax.fori_loop(0, qi, lambda ki, c: step(ki, c, False), init)
    _, l, acc = step(qi, carry, True)
    out = acc / l
    out = out * lax.rsqrt(jnp.mean(out * out, axis=-1, keepdims=True) + EPS) * gain_ref[0]
    o_ref[...] = out.astype(BF16)


def _fox_attn(p, frow3, fref4, gain3):
    s = p.shape[0]
    bq = FOX_BQ
    qcol = 3072 // FOX_HEAD_DIM
    kcol = 4096 // FOX_HEAD_DIM
    vcol = 5120 // FOX_HEAD_DIM
    return pl.pallas_call(
        _fox_kernel,
        grid=(FOX_HEADS, s // bq),
        in_specs=[pl.BlockSpec((bq, FOX_HEAD_DIM), lambda h, i: (i, qcol + h)),
                  pl.BlockSpec((s, FOX_HEAD_DIM), lambda h, i: (0, kcol + h)),
                  pl.BlockSpec((s, FOX_HEAD_DIM), lambda h, i: (0, vcol + h)),
                  pl.BlockSpec((1, 1, s), lambda h, i: (h, 0, 0)),
                  pl.BlockSpec((1, 1, 1, bq), lambda h, i: (h, i, 0, 0)),
                  pl.BlockSpec((1, 1, FOX_HEAD_DIM), lambda h, i: (h, 0, 0))],
        out_specs=pl.BlockSpec((bq, FOX_HEAD_DIM), lambda h, i: (i, h)),
        out_shape=jax.ShapeDtypeStruct((s, FOX_WIDTH), BF16),
        compiler_params=_params(2, 40),
        name="fox_attn",
    )(p, p, p, frow3, fref4, gain3)


def _outrouter_kernel(hm_ref, hf_ref, wt_ref, wb_ref, x_ref, g1_ref, nf_ref, sh2_ref, sc2_ref, wr_ref, br_ref,
                      hres_ref, h2_ref, route_ref, cnt_ref, carry_ref):
    i = pl.program_id(0)
    tm = x_ref.shape[0]

    @pl.when(i == 0)
    def _():
        carry_ref[...] = jnp.zeros_like(carry_ref)

    y = (jnp.dot(hm_ref[...], wt_ref[...], preferred_element_type=F32)
         + jnp.dot(hf_ref[...], wb_ref[...], preferred_element_type=F32))
    hres = x_ref[...] + g1_ref[...] * y
    hres_ref[...] = hres
    ms = jnp.mean(hres * hres, axis=-1, keepdims=True)
    h2 = hres * lax.rsqrt(ms + EPS) * nf_ref[...] * (1.0 + sc2_ref[...]) + sh2_ref[...]

    h2_ref[...] = h2

    logits = jnp.dot(h2, wr_ref[...], precision=HIGHEST, preferred_element_type=F32) + br_ref[...]
    col = lax.broadcasted_iota(I32, logits.shape, 1)
    colf = col.astype(F32)
    lg = jnp.where(col < N_EXPERTS, logits, NEG_INF)
    vals, idxs = [], []
    for _ in range(TOP_K):
        mx = jnp.max(lg, axis=-1, keepdims=True)
        idx = jnp.min(jnp.where(lg == mx, colf, float(LANES)), axis=-1, keepdims=True)
        vals.append(mx)
        idxs.append(idx)
        lg = jnp.where(colf == idx, NEG_INF, lg)
    exps = [jnp.exp(vv - vals[0]) for vv in vals]
    denom = exps[0] + exps[1] + exps[2] + exps[3]
    sel = jnp.zeros(logits.shape, F32)
    for idx in idxs:
        sel = sel + (colf == idx).astype(F32)

    r = lax.broadcasted_iota(I32, (tm, tm), 0)
    cc = lax.broadcasted_iota(I32, (tm, tm), 1)
    tril = (cc <= r).astype(BF16)
    incl = jnp.dot(tril, sel.astype(BF16), preferred_element_type=F32) + carry_ref[...]
    carry_ref[...] = incl[tm - 1:tm, :]
    cnt_ref[...] = incl[tm - 1:tm, :]
    excl = incl - sel

    route = jnp.zeros(logits.shape, F32)
    for kk in range(TOP_K):
        rank = jnp.sum(jnp.where(colf == idxs[kk], excl, 0.0), axis=-1, keepdims=True)
        route = route + jnp.where(col == kk, idxs[kk], 0.0)
        route = route + jnp.where(col == TOP_K + kk, rank, 0.0)
        route = route + jnp.where(col == 2 * TOP_K + kk, exps[kk] / denom, 0.0)
    route_ref[...] = route


def _out_router(hm, hf, wt, wb, x2, g1, nf, sh2, sc2, wr, br):
    s, d = x2.shape
    tm = OUT_TM
    row = lambda i: (0, 0)
    return pl.pallas_call(
        _outrouter_kernel,
        grid=(s // tm,),
        in_specs=[pl.BlockSpec((tm, MLSTM_WIDTH), lambda i: (i, 0)),
                  pl.BlockSpec((tm, FOX_WIDTH), lambda i: (i, 0)),
                  pl.BlockSpec((MLSTM_WIDTH, d), row),
                  pl.BlockSpec((FOX_WIDTH, d), row),
                  pl.BlockSpec((tm, d), lambda i: (i, 0)),
                  pl.BlockSpec((1, d), row), pl.BlockSpec((1, d), row),
                  pl.BlockSpec((1, d), row), pl.BlockSpec((1, d), row),
                  pl.BlockSpec((d, LANES), row), pl.BlockSpec((1, LANES), row)],
        out_specs=[pl.BlockSpec((tm, d), lambda i: (i, 0)),
                   pl.BlockSpec((tm, d), lambda i: (i, 0)),
                   pl.BlockSpec((tm, LANES), lambda i: (i, 0)),
                   pl.BlockSpec((1, LANES), row)],
        out_shape=[jax.ShapeDtypeStruct((s, d), F32),
                   jax.ShapeDtypeStruct((s, d), F32),
                   jax.ShapeDtypeStruct((s, LANES), F32),
                   jax.ShapeDtypeStruct((1, LANES), F32)],
        scratch_shapes=[pltpu.VMEM((1, LANES), F32)],
        compiler_params=_params(1, 48),
        name="out_router",
    )(hm, hf, wt, wb, x2, g1, nf, sh2, sc2, wr, br)


def _start_row_copy(src_hbm, src_row, dst_hbm, dst_row, sem):
    piece = D_MODEL // ROW_PIECES
    for p in range(ROW_PIECES):
        pltpu.make_async_copy(src_hbm.at[pl.ds(src_row, 1), pl.ds(p * piece, piece)],
                              dst_hbm.at[pl.ds(dst_row, 1), pl.ds(p * piece, piece)], sem).start()


def _dispatch_kernel(zstart_ref, zflag_ref, dest_ref, h2_hbm, xs_hbm, zbuf, zsem, sem):
    i = pl.program_id(0)
    n = PAIRS_PER_STEP

    @pl.when(i == 0)
    def _():
        zbuf[...] = jnp.zeros_like(zbuf)

        def zero_copy(e):
            z0 = pl.multiple_of(zstart_ref[e], ZERO_ROWS)
            return pltpu.make_async_copy(zbuf, xs_hbm.at[pl.ds(z0, ZERO_ROWS)], zsem)

        for e in range(N_EXPERTS):
            @pl.when(zflag_ref[e] == 1)
            def _():
                zero_copy(e).start()
        for e in range(N_EXPERTS):
            @pl.when(zflag_ref[e] == 1)
            def _():
                zero_copy(e).wait()

    def issue(j, _):
        tok = lax.shift_right_logical(i * n + j, 2)
        _start_row_copy(h2_hbm, tok, xs_hbm, dest_ref[0, 0, j], sem)
        return 0

    lax.fori_loop(0, n, issue, 0, unroll=8)
    pltpu.make_async_copy(h2_hbm.at[pl.ds(0, n)], xs_hbm.at[pl.ds(0, n)], sem).wait()


def _dispatch(zstart, zflag, dest3, h2, n_rows):
    steps = dest3.shape[0]
    grid_spec = pltpu.PrefetchScalarGridSpec(
        num_scalar_prefetch=2,
        grid=(steps,),
        in_specs=[pl.BlockSpec((1, 1, PAIRS_PER_STEP), lambda i, zs, zf: (i, 0, 0), memory_space=pltpu.SMEM),
                  pl.BlockSpec(memory_space=pl.ANY)],
        out_specs=pl.BlockSpec(memory_space=pl.ANY),
        scratch_shapes=[pltpu.VMEM((ZERO_ROWS, D_MODEL), F32),
                        pltpu.SemaphoreType.DMA(()),
                        pltpu.SemaphoreType.DMA(())],
    )
    return pl.pallas_call(
        _dispatch_kernel,
        grid_spec=grid_spec,
        out_shape=jax.ShapeDtypeStruct((n_rows, D_MODEL), F32),
        compiler_params=_params(1, 16),
        name="dispatch",
    )(zstart, zflag, dest3, h2)


def _expert_kernel(ce_ref, cn_ref, cmap_ref, nact_ref,
                   x_ref, wg_ref, wu_ref, wd_ref, bg_ref, bu_ref, bd_ref, o_ref,
                   wgb_ref, wub_ref, wdb_ref):
    c = pl.program_id(0)
    f = pl.program_id(1)

    @pl.when(c < nact_ref[0])
    def _():
        @pl.when(f == 0)
        def _():
            o_ref[...] = jnp.broadcast_to(bd_ref[...], o_ref.shape)

        wgb_ref[...] = wg_ref[...].astype(BF16)
        wub_ref[...] = wu_ref[...].astype(BF16)
        wdb_ref[...] = wd_ref[...].astype(BF16)
        bg = bg_ref[...]
        bu = bu_ref[...]

        def sub_block(r0, rows):
            xb = x_ref[pl.ds(r0, rows), :].astype(BF16)
            gt = jnp.dot(xb, wgb_ref[...], preferred_element_type=F32) + bg
            up = jnp.dot(xb, wub_ref[...], preferred_element_type=F32) + bu
            gate = jnp.minimum(gt, SWIGLU_LIMIT)
            up = jnp.clip(up, -SWIGLU_LIMIT, SWIGLU_LIMIT)
            act = (up + 1.0) * gate * _sigmoid(SWIGLU_ALPHA * gate)
            o_ref[pl.ds(r0, rows), :] += jnp.dot(act.astype(BF16), wdb_ref[...], preferred_element_type=F32)

        n = cn_ref[c]
        shift = SUB_ROWS.bit_length() - 1
        n_full = lax.shift_right_logical(n, shift)
        rem = n & (SUB_ROWS - 1)
        n_big = n_full + (rem > SUB_ROWS // 2).astype(I32)

        def pair(p, _):
            r0 = pl.multiple_of(p * (2 * SUB_ROWS), 2 * SUB_ROWS)
            sub_block(r0, SUB_ROWS)
            sub_block(r0 + SUB_ROWS, SUB_ROWS)
            return 0

        lax.fori_loop(0, lax.shift_right_logical(n_big, 1), pair, 0)

        @pl.when((n_big & 1) == 1)
        def _():
            sub_block(pl.multiple_of((n_big - 1) * SUB_ROWS, SUB_ROWS), SUB_ROWS)

        @pl.when(jnp.logical_and(rem > 0, rem <= SUB_ROWS // 2))
        def _():
            sub_block(pl.multiple_of(n_full * SUB_ROWS, SUB_ROWS), SUB_ROWS // 2)


def _experts(ce, cn, cmap, nact, xs, w_up_gate, w_down, b_up_gate3, b_down3, n_chunks):
    rc, tf = ROWS_PER_CHUNK, FF_TILE
    nf = D_FF // tf
    d = D_MODEL

    def fsel(c, f, nact_ref):
        return jnp.where(c < nact_ref[0], f, nf - 1)

    grid_spec = pltpu.PrefetchScalarGridSpec(
        num_scalar_prefetch=4,
        grid=(n_chunks, nf),
        in_specs=[
            pl.BlockSpec((rc, d), lambda c, f, ce, cn, cm, na: (cm[c], 0)),
            pl.BlockSpec((None, d, tf), lambda c, f, ce, cn, cm, na: (ce[c], 0, fsel(c, f, na))),
            pl.BlockSpec((None, d, tf), lambda c, f, ce, cn, cm, na: (ce[c], 0, nf + fsel(c, f, na))),
            pl.BlockSpec((None, tf, d), lambda c, f, ce, cn, cm, na: (ce[c], fsel(c, f, na), 0)),
            pl.BlockSpec((None, 1, tf), lambda c, f, ce, cn, cm, na: (ce[c], 0, fsel(c, f, na))),
            pl.BlockSpec((None, 1, tf), lambda c, f, ce, cn, cm, na: (ce[c], 0, nf + fsel(c, f, na))),
            pl.BlockSpec((None, 1, d), lambda c, f, ce, cn, cm, na: (ce[c], 0, 0)),
        ],
        out_specs=pl.BlockSpec((rc, d), lambda c, f, ce, cn, cm, na: (cm[c], 0)),
        scratch_shapes=[pltpu.VMEM((d, tf), BF16), pltpu.VMEM((d, tf), BF16), pltpu.VMEM((tf, d), BF16)],
    )
    return pl.pallas_call(
        _expert_kernel,
        grid_spec=grid_spec,
        out_shape=jax.ShapeDtypeStruct((n_chunks * rc, d), F32),
        compiler_params=_params(2, 58),
        name="experts",
    )(ce, cn, cmap, nact, xs, w_up_gate, w_up_gate, w_down, b_up_gate3, b_up_gate3, b_down3)


def _final_kernel(dcur_ref, dnext_ref, ys_hbm, hres_ref, route_ref, g2_ref, nf_ref, shf_ref, scf_ref, o_ref,
                  buf, sems):
    i = pl.program_id(0)
    tm = FINAL_TM
    slot = i & 1

    def issue(d_ref, dst_slot):
        def body(j, _):
            tok = lax.shift_right_logical(j, 2)
            k = j & (TOP_K - 1)
            pltpu.make_async_copy(ys_hbm.at[pl.ds(d_ref[0, 0, j], 1)],
                                  buf.at[dst_slot, k, pl.ds(tok, 1)], sems.at[dst_slot]).start()
            return 0
        lax.fori_loop(0, TOP_K * tm, body, 0, unroll=8)

    @pl.when(i == 0)
    def _():
        issue(dcur_ref, 0)

    @pl.when(i + 1 < pl.num_programs(0))
    def _():
        issue(dnext_ref, 1 - slot)

    for k in range(TOP_K):
        pltpu.make_async_copy(ys_hbm.at[pl.ds(0, tm)], buf.at[slot, k], sems.at[slot]).wait()

    route = route_ref[...]
    ffn = buf[slot, 0] * route[:, 2 * TOP_K:2 * TOP_K + 1]
    for k in range(1, TOP_K):
        ffn = ffn + buf[slot, k] * route[:, 2 * TOP_K + k:2 * TOP_K + k + 1]
    h = hres_ref[...] + g2_ref[...] * ffn
    ms = jnp.mean(h * h, axis=-1, keepdims=True)
    o_ref[...] = h * lax.rsqrt(ms + EPS) * nf_ref[...] * (1.0 + scf_ref[...]) + shf_ref[...]


def _final(dest3, ys, hres, route, g2, nf, shf, scf):
    s, d = hres.shape
    tm = FINAL_TM
    steps = s // tm
    row = lambda i: (0, 0)
    return pl.pallas_call(
        _final_kernel,
        grid=(steps,),
        in_specs=[pl.BlockSpec((1, 1, TOP_K * tm), lambda i: (i, 0, 0), memory_space=pltpu.SMEM),
                  pl.BlockSpec((1, 1, TOP_K * tm), lambda i: (jnp.minimum(i + 1, steps - 1), 0, 0),
                               memory_space=pltpu.SMEM),
                  pl.BlockSpec(memory_space=pl.ANY),
                  pl.BlockSpec((tm, d), lambda i: (i, 0)),
                  pl.BlockSpec((tm, LANES), lambda i: (i, 0)),
                  pl.BlockSpec((1, d), row), pl.BlockSpec((1, d), row),
                  pl.BlockSpec((1, d), row), pl.BlockSpec((1, d), row)],
        out_specs=pl.BlockSpec((tm, d), lambda i: (i, 0)),
        out_shape=jax.ShapeDtypeStruct((s, d), F32),
        scratch_shapes=[pltpu.VMEM((2, TOP_K, tm, d), F32), pltpu.SemaphoreType.DMA((2,))],
        compiler_params=_params(1, 40),
        name="final",
    )(dest3, dest3, ys, hres, route, g2, nf, shf, scf)


def kernel(x, c, w_ada, b_ada, norm_mix, w_in, b_i, b_f, fox_b_f, fox_q_norm, fox_k_norm, mlstm_out_norm,
           fox_out_norm, w_out, norm_ffn, w_router, b_router, w_up_gate, b_up_gate, w_down, b_down,
           w_ada_final, b_ada_final, norm_final):
    b, s, d = x.shape
    assert b == 1 and d == D_MODEL and w_ada.shape[0] == 1
    x2 = x.reshape(s, d).astype(F32)
    c_col = c.astype(F32).reshape(d, 1)

    mod = _ada_mod(c_col, w_ada[0], b_ada[0].reshape(1, -1))
    sh1, sc1, g1, sh2, sc2, g2 = [mod[:, i * d:(i + 1) * d] for i in range(6)]
    modf = _ada_mod(c_col, w_ada_final, b_ada_final.reshape(1, -1))
    shf, scf = modf[:, :d], modf[:, d:]

    w = w_in[0]
    o_gate = 2 * MLSTM_HEADS * MLSTM_DQK + 2 * MLSTM_WIDTH
    o_fq = o_gate + 2 * MLSTM_HEADS
    o_ff = o_fq + 3 * FOX_WIDTH
    wp = jnp.concatenate([w[:, :o_gate], w[:, o_fq:o_ff]], axis=1).astype(BF16)
    wg = jnp.concatenate([w[:, o_gate:o_fq], w[:, o_ff:o_ff + FOX_HEADS],
                          jnp.zeros((d, LANES - 16), F32)], axis=1).astype(BF16)
    qscale = MLSTM_DQK ** -0.5
    fscale = FOX_HEAD_DIM ** -0.5
    ones = lambda n: jnp.ones((n,), F32)
    colgain = jnp.concatenate([
        ones(MLSTM_HEADS * MLSTM_DQK) * qscale, ones(MLSTM_HEADS * MLSTM_DQK), ones(2 * MLSTM_WIDTH),
        jnp.tile(fox_q_norm[0].astype(F32), FOX_HEADS) * fscale, jnp.tile(fox_k_norm[0].astype(F32), FOX_HEADS),
        ones(FOX_WIDTH)]).reshape(1, P_WIDTH)

    p, gates = _in_proj(x2, norm_mix[0].reshape(1, d), sh1, sc1, wp, wg, colgain)

    gate_bias = jnp.concatenate([b_i[0], b_f[0], fox_b_f[0], jnp.zeros((LANES - 16,), F32)]).reshape(1, LANES)
    a = _gates(gates, gate_bias)
    at = a[:, :16].T

    hm = _mlstm(p, a, at, mlstm_out_norm[0].reshape(1, MLSTM_WIDTH).astype(F32))

    nq = s // FOX_BQ
    frow = at[2 * MLSTM_HEADS:2 * MLSTM_HEADS + FOX_HEADS]
    fref4 = jnp.broadcast_to(frow[:, ::FOX_BQ][:, :, None, None], (FOX_HEADS, nq, 1, FOX_BQ))
    hf = _fox_attn(p, frow.reshape(FOX_HEADS, 1, s), fref4,
                   fox_out_norm[0].reshape(FOX_HEADS, 1, FOX_HEAD_DIM).astype(F32))

    wo = w_out[0].astype(BF16)
    wr = jnp.concatenate([w_router[0].astype(F32), jnp.zeros((d, LANES - N_EXPERTS), F32)], axis=1)
    br = jnp.concatenate([b_router[0].astype(F32), jnp.zeros((LANES - N_EXPERTS,), F32)]).reshape(1, LANES)
    hres, h2, route, cnt = _out_router(hm, hf, wo[:MLSTM_WIDTH], wo[MLSTM_WIDTH:], x2, g1,
                                        norm_ffn[0].reshape(1, d), sh2, sc2, wr, br)

    rc = ROWS_PER_CHUNK
    n_chunks = N_EXPERTS + (s * TOP_K) // rc
    counts = cnt[0, :N_EXPERTS].astype(I32)
    eidx = route[:, :TOP_K].astype(I32)
    rank = route[:, TOP_K:2 * TOP_K].astype(I32)
    nch = (counts + rc - 1) // rc
    cend = jnp.cumsum(nch)
    cstart = cend - nch
    nact = cend[-1]
    dest = cstart[eidx] * rc + rank
    cidx = jnp.arange(n_chunks, dtype=I32)
    cmap = jnp.minimum(cidx, nact - 1)
    ce = jnp.minimum(jnp.sum((cmap[:, None] >= cend[None, :]).astype(I32), axis=1), N_EXPERTS - 1)
    cn = jnp.where(cidx < nact, jnp.clip(counts[ce] - (cmap - cstart[ce]) * rc, 0, rc), 0).astype(I32)
    zflag = (counts % ZERO_ROWS != 0).astype(I32)
    zstart = (cstart * rc + (counts // ZERO_ROWS) * ZERO_ROWS).astype(I32)

    xs = _dispatch(zstart, zflag, dest.reshape(-1, 1, PAIRS_PER_STEP), h2, n_chunks * rc)
    ys = _experts(ce, cn, cmap, nact.reshape(1).astype(I32), xs, w_up_gate[0], w_down[0],
                  b_up_gate[0].reshape(N_EXPERTS, 1, 2 * D_FF), b_down[0].reshape(N_EXPERTS, 1, d), n_chunks)
    out = _final(dest.reshape(-1, 1, TOP_K * FINAL_TM), ys, hres, route, g2, norm_final.reshape(1, d), shf, scf)
    return out.reshape(b, s, d).astype(x.dtype)
```

```python
import jax
import jax.numpy as jnp
from jax import lax
from jax.experimental import pallas as pl
from jax.experimental.pallas import tpu as pltpu

F32 = jnp.float32
BF16 = jnp.bfloat16
I32 = jnp.int32
NEG_INF = float("-inf")
HIGHEST = lax.Precision.HIGHEST

D_MODEL = 2048
MLSTM_HEADS = 4
MLSTM_DQK = 128
MLSTM_DV = 256
MLSTM_WIDTH = MLSTM_HEADS * MLSTM_DV
GATE_SOFTCAP = 15.0
FOX_HEADS = 8
FOX_HEAD_DIM = 128
FOX_WIDTH = FOX_HEADS * FOX_HEAD_DIM
N_EXPERTS = 32
TOP_K = 4
D_FF = 2048
SWIGLU_LIMIT = 7.0
SWIGLU_ALPHA = 1.702
EPS = 1e-6

LANES = 128
P_WIDTH = 6144

ADA_TN = 512
INPROJ_TM = 1024
INPROJ_TN = 512
GATES_TL = 512
MLSTM_L = 256
FOX_BQ = 512
OUT_TM = 256
ROWS_PER_CHUNK = 1152
SUB_ROWS = 256
ZERO_ROWS = SUB_ROWS // 2
HALF = D_MODEL // 2
FF_TILE = 256
PAIRS_PER_STEP = 2048
FINAL_TM = 256
MIB = 1024 * 1024


def _params(n_axes, vmem_mib):
    return pltpu.CompilerParams(dimension_semantics=("arbitrary",) * n_axes,
                                vmem_limit_bytes=vmem_mib * MIB)


def _log_sigmoid(z):
    return jnp.minimum(z, 0.0) - jnp.log1p(jnp.exp(-jnp.abs(z)))


def _sigmoid(z):
    return 1.0 / (1.0 + jnp.exp(-z))


def _ada_kernel(c_ref, w_ref, b_ref, o_ref):
    c = c_ref[...]
    ca = c * _sigmoid(c)
    o_ref[...] = jnp.sum(ca * w_ref[...], axis=0, keepdims=True) + b_ref[...]


def _ada_mod(c_col, w, b_row):
    d, n = w.shape
    return pl.pallas_call(
        _ada_kernel,
        grid=(n // ADA_TN,),
        in_specs=[pl.BlockSpec((d, 1), lambda j: (0, 0)),
                  pl.BlockSpec((d, ADA_TN), lambda j: (0, j)),
                  pl.BlockSpec((1, ADA_TN), lambda j: (0, j))],
        out_specs=pl.BlockSpec((1, ADA_TN), lambda j: (0, j)),
        out_shape=jax.ShapeDtypeStruct((1, n), F32),
        compiler_params=_params(1, 32),
        name="ada_mod",
    )(c_col, w, b_row)


_FQ_COL = 2 * MLSTM_HEADS * MLSTM_DQK + 2 * MLSTM_WIDTH
_NORM_TILE_LO = _FQ_COL // INPROJ_TN
_NORM_TILE_HI = (_FQ_COL + 2 * FOX_WIDTH) // INPROJ_TN


def _inproj_kernel(x_ref, g_ref, sh_ref, sc_ref, w_ref, wg_ref, cg_ref, p_ref, gate_ref, hn_ref):
    j = pl.program_id(1)

    @pl.when(j == 0)
    def _():
        x = x_ref[...]
        ms = jnp.mean(x * x, axis=-1, keepdims=True)
        hn = x * lax.rsqrt(ms + EPS) * g_ref[...] * (1.0 + sc_ref[...]) + sh_ref[...]
        hnb = hn.astype(BF16)
        hn_ref[...] = hnb
        gate_ref[...] = jnp.dot(hnb, wg_ref[...], preferred_element_type=F32)

    acc = jnp.dot(hn_ref[...], w_ref[...], preferred_element_type=F32)
    is_norm = jnp.logical_and(j >= _NORM_TILE_LO, j < _NORM_TILE_HI)

    @pl.when(is_norm)
    def _():
        parts = []
        for hh in range(INPROJ_TN // FOX_HEAD_DIM):
            a = acc[:, hh * FOX_HEAD_DIM:(hh + 1) * FOX_HEAD_DIM]
            parts.append(a * lax.rsqrt(jnp.mean(a * a, axis=-1, keepdims=True) + EPS))
        p_ref[...] = (jnp.concatenate(parts, axis=-1) * cg_ref[...]).astype(BF16)

    @pl.when(jnp.logical_not(is_norm))
    def _():
        p_ref[...] = (acc * cg_ref[...]).astype(BF16)


def _in_proj(x2, g_row, sh_row, sc_row, wp, wg, colgain):
    s, d = x2.shape
    tm, tn = INPROJ_TM, INPROJ_TN
    return pl.pallas_call(
        _inproj_kernel,
        grid=(s // tm, P_WIDTH // tn),
        in_specs=[pl.BlockSpec((tm, d), lambda i, j: (i, 0)),
                  pl.BlockSpec((1, d), lambda i, j: (0, 0)),
                  pl.BlockSpec((1, d), lambda i, j: (0, 0)),
                  pl.BlockSpec((1, d), lambda i, j: (0, 0)),
                  pl.BlockSpec((d, tn), lambda i, j: (0, j)),
                  pl.BlockSpec((d, LANES), lambda i, j: (0, 0)),
                  pl.BlockSpec((1, tn), lambda i, j: (0, j))],
        out_specs=[pl.BlockSpec((tm, tn), lambda i, j: (i, j)),
                   pl.BlockSpec((tm, LANES), lambda i, j: (i, 0))],
        out_shape=[jax.ShapeDtypeStruct((s, P_WIDTH), BF16),
                   jax.ShapeDtypeStruct((s, LANES), F32)],
        scratch_shapes=[pltpu.VMEM((tm, d), BF16)],
        compiler_params=_params(2, 48),
        name="in_proj",
    )(x2, g_row, sh_row, sc_row, wp, wg, colgain)


def _gates_kernel(g_ref, b_ref, a_ref, carry_ref):
    i = pl.program_id(0)

    @pl.when(i == 0)
    def _():
        carry_ref[...] = jnp.zeros_like(carry_ref)

    tl = g_ref.shape[0]
    z = g_ref[...] + b_ref[...]
    col = lax.broadcasted_iota(I32, z.shape, 1)
    capped = GATE_SOFTCAP * jnp.tanh(z / GATE_SOFTCAP)
    r = lax.broadcasted_iota(I32, (tl, tl), 0)
    cc = lax.broadcasted_iota(I32, (tl, tl), 1)
    tril = (cc <= r).astype(F32)
    cs = jnp.dot(tril, _log_sigmoid(z), precision=HIGHEST, preferred_element_type=F32) + carry_ref[...]
    carry_ref[...] = cs[tl - 1:tl, :]
    a_ref[...] = jnp.where(col < MLSTM_HEADS, capped,
                           jnp.where(col < 2 * MLSTM_HEADS, _log_sigmoid(capped), cs))


def _gates(gates, bias_row):
    s = gates.shape[0]
    tl = GATES_TL
    return pl.pallas_call(
        _gates_kernel,
        grid=(s // tl,),
        in_specs=[pl.BlockSpec((tl, LANES), lambda i: (i, 0)),
                  pl.BlockSpec((1, LANES), lambda i: (0, 0))],
        out_specs=pl.BlockSpec((tl, LANES), lambda i: (i, 0)),
        out_shape=jax.ShapeDtypeStruct((s, LANES), F32),
        scratch_shapes=[pltpu.VMEM((1, LANES), F32)],
        compiler_params=_params(1, 32),
        name="gates",
    )(gates, bias_row)


def _mlstm_kernel(q_ref, k_ref, v_ref, mo_ref, a_ref, at_ref, gain_ref, o_ref, c_ref, n_ref, m_ref):
    ci = pl.program_id(0)
    L = MLSTM_L

    @pl.when(ci == 0)
    def _():
        c_ref[...] = jnp.zeros_like(c_ref)
        n_ref[...] = jnp.zeros_like(n_ref)
        m_ref[...] = jnp.zeros_like(m_ref)

    r = lax.broadcasted_iota(I32, (L, L), 0)
    cc = lax.broadcasted_iota(I32, (L, L), 1)
    causal = cc <= r
    tril = causal.astype(F32)
    a = a_ref[...]
    at = at_ref[...]
    g_cols = jnp.dot(tril, a, precision=HIGHEST, preferred_element_type=F32)
    g_rows = jnp.dot(at, (r <= cc).astype(F32), precision=HIGHEST, preferred_element_type=F32)

    for hh in range(MLSTM_HEADS):
        q = q_ref[:, hh * MLSTM_DQK:(hh + 1) * MLSTM_DQK]
        k = k_ref[:, hh * MLSTM_DQK:(hh + 1) * MLSTM_DQK]
        v = v_ref[:, hh * MLSTM_DV:(hh + 1) * MLSTM_DV]
        i_col = a[:, hh:hh + 1]
        i_row = at[hh:hh + 1, :]
        g_col = g_cols[:, MLSTM_HEADS + hh:MLSTM_HEADS + hh + 1]
        g_row = g_rows[MLSTM_HEADS + hh:MLSTM_HEADS + hh + 1, :]
        m_prev = m_ref[hh][:, 0:1]
        c_prev = c_ref[hh]
        n_prev = n_ref[hh]

        m_inter = g_col + m_prev
        dmat = jnp.where(causal, g_col - g_row + i_row, NEG_INF)
        m_t = jnp.maximum(m_inter, jnp.max(dmat, axis=-1, keepdims=True))
        qk = lax.dot_general(q, k, (((1,), (1,)), ((), ())), preferred_element_type=F32)
        scores = qk * jnp.exp(dmat - m_t)
        inter = jnp.exp(m_inter - m_t)
        q_c = jnp.dot(q, c_prev.astype(BF16), preferred_element_type=F32)
        num = jnp.dot(scores.astype(BF16), v, preferred_element_type=F32) + inter * q_c
        q_n = jnp.sum(q.astype(F32) * n_prev, axis=-1, keepdims=True)
        den = jnp.sum(scores, axis=-1, keepdims=True) + inter * q_n
        h_out = num / jnp.maximum(jnp.abs(den), jnp.exp(-m_t))

        g_last = g_col[L - 1:L, :]
        log_w = g_last - g_col + i_col
        m_new = jnp.maximum(g_last + m_prev, jnp.max(log_w, axis=0, keepdims=True))
        w_col = jnp.exp(log_w - m_new)
        decay = jnp.exp(g_last + m_prev - m_new)
        kw = k.astype(F32) * w_col
        upd = jnp.dot(kw.T.astype(BF16), v, preferred_element_type=F32)
        c_ref[hh] = decay * c_prev + upd
        n_ref[hh] = decay * n_prev + jnp.sum(kw, axis=0, keepdims=True)
        m_ref[hh] = jnp.broadcast_to(m_new, (1, LANES))

        hn = h_out * lax.rsqrt(jnp.mean(h_out * h_out, axis=-1, keepdims=True) + EPS)
        hn = hn * gain_ref[:, hh * MLSTM_DV:(hh + 1) * MLSTM_DV]
        mo = mo_ref[:, hh * MLSTM_DV:(hh + 1) * MLSTM_DV].astype(F32)
        o_ref[:, hh * MLSTM_DV:(hh + 1) * MLSTM_DV] = (hn * _sigmoid(mo)).astype(BF16)


def _mlstm(p, a, at, gain_row):
    s = p.shape[0]
    L = MLSTM_L
    qw = MLSTM_HEADS * MLSTM_DQK
    return pl.pallas_call(
        _mlstm_kernel,
        grid=(s // L,),
        in_specs=[pl.BlockSpec((L, qw), lambda i: (i, 0)),
                  pl.BlockSpec((L, qw), lambda i: (i, 1)),
                  pl.BlockSpec((L, MLSTM_WIDTH), lambda i: (i, 1)),
                  pl.BlockSpec((L, MLSTM_WIDTH), lambda i: (i, 2)),
                  pl.BlockSpec((L, LANES), lambda i: (i, 0)),
                  pl.BlockSpec((16, L), lambda i: (0, i)),
                  pl.BlockSpec((1, MLSTM_WIDTH), lambda i: (0, 0))],
        out_specs=pl.BlockSpec((L, MLSTM_WIDTH), lambda i: (i, 0)),
        out_shape=jax.ShapeDtypeStruct((s, MLSTM_WIDTH), BF16),
        scratch_shapes=[pltpu.VMEM((MLSTM_HEADS, MLSTM_DQK, MLSTM_DV), F32),
                        pltpu.VMEM((MLSTM_HEADS, 1, MLSTM_DQK), F32),
                        pltpu.VMEM((MLSTM_HEADS, 1, LANES), F32)],
        compiler_params=_params(1, 32),
        name="mlstm",
    )(p, p, p, p, a, at, gain_row)


def _fox_kernel(q_ref, k_ref, v_ref, frow_ref, fref_ref, gain_ref, o_ref):
    qi = pl.program_id(1)
    bq = FOX_BQ
    q = q_ref[...]
    fref = fref_ref[0, 0]

    def step(ki, carry, masked):
        m, l, acc = carry
        off = pl.multiple_of(ki * bq, bq)
        k = k_ref[pl.ds(off, bq), :]
        v = v_ref[pl.ds(off, bq), :]
        s = lax.dot_general(q, k, (((1,), (1,)), ((), ())), preferred_element_type=F32)
        s = s + (fref - frow_ref[0, :, pl.ds(off, bq)])
        if masked:
            r = lax.broadcasted_iota(I32, (bq, bq), 0)
            c = lax.broadcasted_iota(I32, (bq, bq), 1)
            s = jnp.where(c <= r, s, NEG_INF)
        m_new = jnp.maximum(m, jnp.max(s, axis=-1, keepdims=True))
        alpha = jnp.exp(m - m_new)
        p = jnp.exp(s - m_new)
        l = alpha * l + jnp.sum(p, axis=-1, keepdims=True)
        acc = alpha * acc + jnp.dot(p.astype(BF16), v, preferred_element_type=F32)
        return m_new, l, acc

    init = (jnp.full((bq, 1), NEG_INF, F32), jnp.zeros((bq, 1), F32), jnp.zeros((bq, FOX_HEAD_DIM), F32))
    carry = lax.fori_loop(0, qi, lambda ki, c: step(ki, c, False), init)
    _, l, acc = step(qi, carry, True)
    out = acc / l
    out = out * lax.rsqrt(jnp.mean(out * out, axis=-1, keepdims=True) + EPS) * gain_ref[0]
    o_ref[...] = out.astype(BF16)


def _fox_attn(p, frow3, fref4, gain3):
    s = p.shape[0]
    bq = FOX_BQ
    qcol = 3072 // FOX_HEAD_DIM
    kcol = 4096 // FOX_HEAD_DIM
    vcol = 5120 // FOX_HEAD_DIM
    return pl.pallas_call(
        _fox_kernel,
        grid=(FOX_HEADS, s // bq),
        in_specs=[pl.BlockSpec((bq, FOX_HEAD_DIM), lambda h, i: (i, qcol + h)),
                  pl.BlockSpec((s, FOX_HEAD_DIM), lambda h, i: (0, kcol + h)),
                  pl.BlockSpec((s, FOX_HEAD_DIM), lambda h, i: (0, vcol + h)),
                  pl.BlockSpec((1, 1, s), lambda h, i: (h, 0, 0)),
                  pl.BlockSpec((1, 1, 1, bq), lambda h, i: (h, i, 0, 0)),
                  pl.BlockSpec((1, 1, FOX_HEAD_DIM), lambda h, i: (h, 0, 0))],
        out_specs=pl.BlockSpec((bq, FOX_HEAD_DIM), lambda h, i: (i, h)),
        out_shape=jax.ShapeDtypeStruct((s, FOX_WIDTH), BF16),
        compiler_params=_params(2, 40),
        name="fox_attn",
    )(p, p, p, frow3, fref4, gain3)


def _outrouter_kernel(hm_ref, hf_ref, wt_ref, wb_ref, x_ref, g1_ref, nf_ref, sh2_ref, sc2_ref, wr_ref, br_ref,
                      hres_ref, h2a_ref, h2b_ref, route_ref, cnt_ref, carry_ref):
    i = pl.program_id(0)
    tm = x_ref.shape[0]

    @pl.when(i == 0)
    def _():
        carry_ref[...] = jnp.zeros_like(carry_ref)

    y = (jnp.dot(hm_ref[...], wt_ref[...], preferred_element_type=F32)
         + jnp.dot(hf_ref[...], wb_ref[...], preferred_element_type=F32))
    hres = x_ref[...] + g1_ref[...] * y
    hres_ref[...] = hres
    ms = jnp.mean(hres * hres, axis=-1, keepdims=True)
    h2 = hres * lax.rsqrt(ms + EPS) * nf_ref[...] * (1.0 + sc2_ref[...]) + sh2_ref[...]

    h2a_ref[...] = h2[:, :HALF]
    h2b_ref[...] = h2[:, HALF:]

    logits = jnp.dot(h2, wr_ref[...], precision=HIGHEST, preferred_element_type=F32) + br_ref[...]
    col = lax.broadcasted_iota(I32, logits.shape, 1)
    colf = col.astype(F32)
    lg = jnp.where(col < N_EXPERTS, logits, NEG_INF)
    vals, idxs = [], []
    for _ in range(TOP_K):
        mx = jnp.max(lg, axis=-1, keepdims=True)
        idx = jnp.min(jnp.where(lg == mx, colf, float(LANES)), axis=-1, keepdims=True)
        vals.append(mx)
        idxs.append(idx)
        lg = jnp.where(colf == idx, NEG_INF, lg)
    exps = [jnp.exp(vv - vals[0]) for vv in vals]
    denom = exps[0] + exps[1] + exps[2] + exps[3]
    sel = jnp.zeros(logits.shape, F32)
    for idx in idxs:
        sel = sel + (colf == idx).astype(F32)

    r = lax.broadcasted_iota(I32, (tm, tm), 0)
    cc = lax.broadcasted_iota(I32, (tm, tm), 1)
    tril = (cc <= r).astype(BF16)
    incl = jnp.dot(tril, sel.astype(BF16), preferred_element_type=F32) + carry_ref[...]
    carry_ref[...] = incl[tm - 1:tm, :]
    cnt_ref[...] = incl[tm - 1:tm, :]
    excl = incl - sel

    route = jnp.zeros(logits.shape, F32)
    for kk in range(TOP_K):
        rank = jnp.sum(jnp.where(colf == idxs[kk], excl, 0.0), axis=-1, keepdims=True)
        route = route + jnp.where(col == kk, idxs[kk], 0.0)
        route = route + jnp.where(col == TOP_K + kk, rank, 0.0)
        route = route + jnp.where(col == 2 * TOP_K + kk, exps[kk] / denom, 0.0)
    route_ref[...] = route


def _out_router(hm, hf, wt, wb, x2, g1, nf, sh2, sc2, wr, br):
    s, d = x2.shape
    tm = OUT_TM
    row = lambda i: (0, 0)
    return pl.pallas_call(
        _outrouter_kernel,
        grid=(s // tm,),
        in_specs=[pl.BlockSpec((tm, MLSTM_WIDTH), lambda i: (i, 0)),
                  pl.BlockSpec((tm, FOX_WIDTH), lambda i: (i, 0)),
                  pl.BlockSpec((MLSTM_WIDTH, d), row),
                  pl.BlockSpec((FOX_WIDTH, d), row),
                  pl.BlockSpec((tm, d), lambda i: (i, 0)),
                  pl.BlockSpec((1, d), row), pl.BlockSpec((1, d), row),
                  pl.BlockSpec((1, d), row), pl.BlockSpec((1, d), row),
                  pl.BlockSpec((d, LANES), row), pl.BlockSpec((1, LANES), row)],
        out_specs=[pl.BlockSpec((tm, d), lambda i: (i, 0)),
                   pl.BlockSpec((tm, HALF), lambda i: (i, 0)),
                   pl.BlockSpec((tm, HALF), lambda i: (i, 0)),
                   pl.BlockSpec((tm, LANES), lambda i: (i, 0)),
                   pl.BlockSpec((1, LANES), row)],
        out_shape=[jax.ShapeDtypeStruct((s, d), F32),
                   jax.ShapeDtypeStruct((s, HALF), F32),
                   jax.ShapeDtypeStruct((s, HALF), F32),
                   jax.ShapeDtypeStruct((s, LANES), F32),
                   jax.ShapeDtypeStruct((1, LANES), F32)],
        scratch_shapes=[pltpu.VMEM((1, LANES), F32)],
        compiler_params=_params(1, 48),
        name="out_router",
    )(hm, hf, wt, wb, x2, g1, nf, sh2, sc2, wr, br)


def _dispatch_kernel(zstart_ref, zflag_ref, dest_ref, h2a_hbm, h2b_hbm, xsa_hbm, xsb_hbm, zbuf, zsem, sem):
    i = pl.program_id(0)
    n = PAIRS_PER_STEP
    halves = ((h2a_hbm, xsa_hbm), (h2b_hbm, xsb_hbm))

    @pl.when(i == 0)
    def _():
        zbuf[...] = jnp.zeros_like(zbuf)

        def zero_copy(e, dst_hbm):
            z0 = pl.multiple_of(zstart_ref[e], ZERO_ROWS)
            return pltpu.make_async_copy(zbuf, dst_hbm.at[pl.ds(z0, ZERO_ROWS)], zsem)

        for e in range(N_EXPERTS):
            @pl.when(zflag_ref[e] == 1)
            def _():
                for _, dst_hbm in halves:
                    zero_copy(e, dst_hbm).start()
        for e in range(N_EXPERTS):
            @pl.when(zflag_ref[e] == 1)
            def _():
                for _, dst_hbm in halves:
                    zero_copy(e, dst_hbm).wait()

    def issue(j, _):
        tok = lax.shift_right_logical(i * n + j, 2)
        row = dest_ref[0, 0, j]
        for src_hbm, dst_hbm in halves:
            pltpu.make_async_copy(src_hbm.at[pl.ds(tok, 1)], dst_hbm.at[pl.ds(row, 1)], sem).start()
        return 0

    lax.fori_loop(0, n, issue, 0, unroll=8)
    for src_hbm, dst_hbm in halves:
        pltpu.make_async_copy(src_hbm.at[pl.ds(0, n)], dst_hbm.at[pl.ds(0, n)], sem).wait()


def _dispatch(zstart, zflag, dest3, h2a, h2b, n_rows):
    steps = dest3.shape[0]
    any_spec = pl.BlockSpec(memory_space=pl.ANY)
    grid_spec = pltpu.PrefetchScalarGridSpec(
        num_scalar_prefetch=2,
        grid=(steps,),
        in_specs=[pl.BlockSpec((1, 1, PAIRS_PER_STEP), lambda i, zs, zf: (i, 0, 0), memory_space=pltpu.SMEM),
                  any_spec, any_spec],
        out_specs=[any_spec, any_spec],
        scratch_shapes=[pltpu.VMEM((ZERO_ROWS, HALF), F32),
                        pltpu.SemaphoreType.DMA(()),
                        pltpu.SemaphoreType.DMA(())],
    )
    return pl.pallas_call(
        _dispatch_kernel,
        grid_spec=grid_spec,
        out_shape=[jax.ShapeDtypeStruct((n_rows, HALF), F32)] * 2,
        compiler_params=_params(1, 16),
        name="dispatch",
    )(zstart, zflag, dest3, h2a, h2b)


def _expert_kernel(ce_ref, cn_ref, cmap_ref, nact_ref,
                   xa_ref, xb_ref, wg_ref, wu_ref, wd_ref, bg_ref, bu_ref, bd_ref, o_ref,
                   wgb_ref, wub_ref, wdb_ref):
    c = pl.program_id(0)
    f = pl.program_id(1)

    @pl.when(c < nact_ref[0])
    def _():
        @pl.when(f == 0)
        def _():
            o_ref[...] = jnp.broadcast_to(bd_ref[...], o_ref.shape)

        wgb_ref[...] = wg_ref[...].astype(BF16)
        wub_ref[...] = wu_ref[...].astype(BF16)
        wdb_ref[...] = wd_ref[...].astype(BF16)
        bg = bg_ref[...]
        bu = bu_ref[...]

        def sub_block(r0, rows):
            xa = xa_ref[pl.ds(r0, rows), :].astype(BF16)
            xb = xb_ref[pl.ds(r0, rows), :].astype(BF16)
            gt = (jnp.dot(xa, wgb_ref[:HALF, :], preferred_element_type=F32)
                  + jnp.dot(xb, wgb_ref[HALF:, :], preferred_element_type=F32) + bg)
            up = (jnp.dot(xa, wub_ref[:HALF, :], preferred_element_type=F32)
                  + jnp.dot(xb, wub_ref[HALF:, :], preferred_element_type=F32) + bu)
            gate = jnp.minimum(gt, SWIGLU_LIMIT)
            up = jnp.clip(up, -SWIGLU_LIMIT, SWIGLU_LIMIT)
            act = (up + 1.0) * gate * _sigmoid(SWIGLU_ALPHA * gate)
            o_ref[pl.ds(r0, rows), :] += jnp.dot(act.astype(BF16), wdb_ref[...], preferred_element_type=F32)

        n = cn_ref[c]
        shift = SUB_ROWS.bit_length() - 1
        n_full = lax.shift_right_logical(n, shift)
        rem = n & (SUB_ROWS - 1)
        n_big = n_full + (rem > SUB_ROWS // 2).astype(I32)

        def pair(p, _):
            r0 = pl.multiple_of(p * (2 * SUB_ROWS), 2 * SUB_ROWS)
            sub_block(r0, SUB_ROWS)
            sub_block(r0 + SUB_ROWS, SUB_ROWS)
            return 0

        lax.fori_loop(0, lax.shift_right_logical(n_big, 1), pair, 0)

        @pl.when((n_big & 1) == 1)
        def _():
            sub_block(pl.multiple_of((n_big - 1) * SUB_ROWS, SUB_ROWS), SUB_ROWS)

        @pl.when(jnp.logical_and(rem > 0, rem <= SUB_ROWS // 2))
        def _():
            sub_block(pl.multiple_of(n_full * SUB_ROWS, SUB_ROWS), SUB_ROWS // 2)


def _experts(ce, cn, cmap, nact, xsa, xsb, w_up_gate, w_down, b_up_gate3, b_down3, n_chunks):
    rc, tf = ROWS_PER_CHUNK, FF_TILE
    nf = D_FF // tf
    d = D_MODEL

    def fsel(c, f, nact_ref):
        return jnp.where(c < nact_ref[0], f, nf - 1)

    grid_spec = pltpu.PrefetchScalarGridSpec(
        num_scalar_prefetch=4,
        grid=(n_chunks, nf),
        in_specs=[
            pl.BlockSpec((rc, HALF), lambda c, f, ce, cn, cm, na: (cm[c], 0)),
            pl.BlockSpec((rc, HALF), lambda c, f, ce, cn, cm, na: (cm[c], 0)),
            pl.BlockSpec((None, d, tf), lambda c, f, ce, cn, cm, na: (ce[c], 0, fsel(c, f, na))),
            pl.BlockSpec((None, d, tf), lambda c, f, ce, cn, cm, na: (ce[c], 0, nf + fsel(c, f, na))),
            pl.BlockSpec((None, tf, d), lambda c, f, ce, cn, cm, na: (ce[c], fsel(c, f, na), 0)),
            pl.BlockSpec((None, 1, tf), lambda c, f, ce, cn, cm, na: (ce[c], 0, fsel(c, f, na))),
            pl.BlockSpec((None, 1, tf), lambda c, f, ce, cn, cm, na: (ce[c], 0, nf + fsel(c, f, na))),
            pl.BlockSpec((None, 1, d), lambda c, f, ce, cn, cm, na: (ce[c], 0, 0)),
        ],
        out_specs=pl.BlockSpec((rc, d), lambda c, f, ce, cn, cm, na: (cm[c], 0)),
        scratch_shapes=[pltpu.VMEM((d, tf), BF16), pltpu.VMEM((d, tf), BF16), pltpu.VMEM((tf, d), BF16)],
    )
    return pl.pallas_call(
        _expert_kernel,
        grid_spec=grid_spec,
        out_shape=jax.ShapeDtypeStruct((n_chunks * rc, d), F32),
        compiler_params=_params(2, 58),
        name="experts",
    )(ce, cn, cmap, nact, xsa, xsb, w_up_gate, w_up_gate, w_down, b_up_gate3, b_up_gate3, b_down3)


def _final_kernel(dcur_ref, dnext_ref, ys_hbm, hres_ref, route_ref, g2_ref, nf_ref, shf_ref, scf_ref, o_ref,
                  buf, sems):
    i = pl.program_id(0)
    tm = FINAL_TM
    slot = i & 1

    def issue(d_ref, dst_slot):
        def body(j, _):
            tok = lax.shift_right_logical(j, 2)
            k = j & (TOP_K - 1)
            pltpu.make_async_copy(ys_hbm.at[pl.ds(d_ref[0, 0, j], 1)],
                                  buf.at[dst_slot, k, pl.ds(tok, 1)], sems.at[dst_slot]).start()
            return 0
        lax.fori_loop(0, TOP_K * tm, body, 0, unroll=8)

    @pl.when(i == 0)
    def _():
        issue(dcur_ref, 0)

    @pl.when(i + 1 < pl.num_programs(0))
    def _():
        issue(dnext_ref, 1 - slot)

    for k in range(TOP_K):
        pltpu.make_async_copy(ys_hbm.at[pl.ds(0, tm)], buf.at[slot, k], sems.at[slot]).wait()

    route = route_ref[...]
    ffn = buf[slot, 0] * route[:, 2 * TOP_K:2 * TOP_K + 1]
    for k in range(1, TOP_K):
        ffn = ffn + buf[slot, k] * route[:, 2 * TOP_K + k:2 * TOP_K + k + 1]
    h = hres_ref[...] + g2_ref[...] * ffn
    ms = jnp.mean(h * h, axis=-1, keepdims=True)
    o_ref[...] = h * lax.rsqrt(ms + EPS) * nf_ref[...] * (1.0 + scf_ref[...]) + shf_ref[...]


def _final(dest3, ys, hres, route, g2, nf, shf, scf):
    s, d = hres.shape
    tm = FINAL_TM
    steps = s // tm
    row = lambda i: (0, 0)
    return pl.pallas_call(
        _final_kernel,
        grid=(steps,),
        in_specs=[pl.BlockSpec((1, 1, TOP_K * tm), lambda i: (i, 0, 0), memory_space=pltpu.SMEM),
                  pl.BlockSpec((1, 1, TOP_K * tm), lambda i: (jnp.minimum(i + 1, steps - 1), 0, 0),
                               memory_space=pltpu.SMEM),
                  pl.BlockSpec(memory_space=pl.ANY),
                  pl.BlockSpec((tm, d), lambda i: (i, 0)),
                  pl.BlockSpec((tm, LANES), lambda i: (i, 0)),
                  pl.BlockSpec((1, d), row), pl.BlockSpec((1, d), row),
                  pl.BlockSpec((1, d), row), pl.BlockSpec((1, d), row)],
        out_specs=pl.BlockSpec((tm, d), lambda i: (i, 0)),
        out_shape=jax.ShapeDtypeStruct((s, d), F32),
        scratch_shapes=[pltpu.VMEM((2, TOP_K, tm, d), F32), pltpu.SemaphoreType.DMA((2,))],
        compiler_params=_params(1, 40),
        name="final",
    )(dest3, dest3, ys, hres, route, g2, nf, shf, scf)


def kernel(x, c, w_ada, b_ada, norm_mix, w_in, b_i, b_f, fox_b_f, fox_q_norm, fox_k_norm, mlstm_out_norm,
           fox_out_norm, w_out, norm_ffn, w_router, b_router, w_up_gate, b_up_gate, w_down, b_down,
           w_ada_final, b_ada_final, norm_final):
    b, s, d = x.shape
    assert b == 1 and d == D_MODEL and w_ada.shape[0] == 1
    x2 = x.reshape(s, d).astype(F32)
    c_col = c.astype(F32).reshape(d, 1)

    mod = _ada_mod(c_col, w_ada[0], b_ada[0].reshape(1, -1))
    sh1, sc1, g1, sh2, sc2, g2 = [mod[:, i * d:(i + 1) * d] for i in range(6)]
    modf = _ada_mod(c_col, w_ada_final, b_ada_final.reshape(1, -1))
    shf, scf = modf[:, :d], modf[:, d:]

    w = w_in[0]
    o_gate = 2 * MLSTM_HEADS * MLSTM_DQK + 2 * MLSTM_WIDTH
    o_fq = o_gate + 2 * MLSTM_HEADS
    o_ff = o_fq + 3 * FOX_WIDTH
    wp = jnp.concatenate([w[:, :o_gate], w[:, o_fq:o_ff]], axis=1).astype(BF16)
    wg = jnp.concatenate([w[:, o_gate:o_fq], w[:, o_ff:o_ff + FOX_HEADS],
                          jnp.zeros((d, LANES - 16), F32)], axis=1).astype(BF16)
    qscale = MLSTM_DQK ** -0.5
    fscale = FOX_HEAD_DIM ** -0.5
    ones = lambda n: jnp.ones((n,), F32)
    colgain = jnp.concatenate([
        ones(MLSTM_HEADS * MLSTM_DQK) * qscale, ones(MLSTM_HEADS * MLSTM_DQK), ones(2 * MLSTM_WIDTH),
        jnp.tile(fox_q_norm[0].astype(F32), FOX_HEADS) * fscale, jnp.tile(fox_k_norm[0].astype(F32), FOX_HEADS),
        ones(FOX_WIDTH)]).reshape(1, P_WIDTH)

    p, gates = _in_proj(x2, norm_mix[0].reshape(1, d), sh1, sc1, wp, wg, colgain)

    gate_bias = jnp.concatenate([b_i[0], b_f[0], fox_b_f[0], jnp.zeros((LANES - 16,), F32)]).reshape(1, LANES)
    a = _gates(gates, gate_bias)
    at = a[:, :16].T

    hm = _mlstm(p, a, at, mlstm_out_norm[0].reshape(1, MLSTM_WIDTH).astype(F32))

    nq = s // FOX_BQ
    frow = at[2 * MLSTM_HEADS:2 * MLSTM_HEADS + FOX_HEADS]
    fref4 = jnp.broadcast_to(frow[:, ::FOX_BQ][:, :, None, None], (FOX_HEADS, nq, 1, FOX_BQ))
    hf = _fox_attn(p, frow.reshape(FOX_HEADS, 1, s), fref4,
                   fox_out_norm[0].reshape(FOX_HEADS, 1, FOX_HEAD_DIM).astype(F32))

    wo = w_out[0].astype(BF16)
    wr = jnp.concatenate([w_router[0].astype(F32), jnp.zeros((d, LANES - N_EXPERTS), F32)], axis=1)
    br = jnp.concatenate([b_router[0].astype(F32), jnp.zeros((LANES - N_EXPERTS,), F32)]).reshape(1, LANES)
    hres, h2a, h2b, route, cnt = _out_router(hm, hf, wo[:MLSTM_WIDTH], wo[MLSTM_WIDTH:], x2, g1,
                                        norm_ffn[0].reshape(1, d), sh2, sc2, wr, br)

    rc = ROWS_PER_CHUNK
    n_chunks = N_EXPERTS + (s * TOP_K) // rc
    counts = cnt[0, :N_EXPERTS].astype(I32)
    eidx = route[:, :TOP_K].astype(I32)
    rank = route[:, TOP_K:2 * TOP_K].astype(I32)
    nch = (counts + rc - 1) // rc
    cend = jnp.cumsum(nch)
    cstart = cend - nch
    nact = cend[-1]
    dest = cstart[eidx] * rc + rank
    cidx = jnp.arange(n_chunks, dtype=I32)
    cmap = jnp.minimum(cidx, nact - 1)
    ce = jnp.minimum(jnp.sum((cmap[:, None] >= cend[None, :]).astype(I32), axis=1), N_EXPERTS - 1)
    cn = jnp.where(cidx < nact, jnp.clip(counts[ce] - (cmap - cstart[ce]) * rc, 0, rc), 0).astype(I32)
    zflag = (counts % ZERO_ROWS != 0).astype(I32)
    zstart = (cstart * rc + (counts // ZERO_ROWS) * ZERO_ROWS).astype(I32)

    xsa, xsb = _dispatch(zstart, zflag, dest.reshape(-1, 1, PAIRS_PER_STEP), h2a, h2b, n_chunks * rc)
    ys = _experts(ce, cn, cmap, nact.reshape(1).astype(I32), xsa, xsb, w_up_gate[0], w_down[0],
                  b_up_gate[0].reshape(N_EXPERTS, 1, 2 * D_FF), b_down[0].reshape(N_EXPERTS, 1, d), n_chunks)
    out = _final(dest.reshape(-1, 1, TOP_K * FINAL_TM), ys, hres, route, g2, norm_final.reshape(1, d), shf, scf)
    return out.reshape(b, s, d).astype(x.dtype)
```

```python
import jax
import jax.numpy as jnp
from jax import lax
from jax.experimental import pallas as pl
from jax.experimental.pallas import tpu as pltpu

F32 = jnp.float32
BF16 = jnp.bfloat16
I32 = jnp.int32
U32 = jnp.uint32
NEG_INF = float("-inf")
HIGHEST = lax.Precision.HIGHEST

D_MODEL = 2048
MLSTM_HEADS = 4
MLSTM_DQK = 128
MLSTM_DV = 256
MLSTM_WIDTH = MLSTM_HEADS * MLSTM_DV
GATE_SOFTCAP = 15.0
FOX_HEADS = 8
FOX_HEAD_DIM = 128
FOX_WIDTH = FOX_HEADS * FOX_HEAD_DIM
N_EXPERTS = 32
TOP_K = 4
D_FF = 2048
SWIGLU_LIMIT = 7.0
SWIGLU_ALPHA = 1.702
EPS = 1e-6

LANES = 128
P_WIDTH = 6144

ADA_TN = 512
INPROJ_TM = 1024
INPROJ_TN = 512
GATES_TL = 512
MLSTM_L = 256
FOX_BQ = 512
OUT_TM = 256
ROWS_PER_CHUNK = 1152
SUB_ROWS = 256
ZERO_ROWS = SUB_ROWS // 2
HALF = D_MODEL // 2
FF_TILE = 256
PAIRS_PER_STEP = 2048
FINAL_TM = 256
MIB = 1024 * 1024


def _params(n_axes, vmem_mib):
    return pltpu.CompilerParams(dimension_semantics=("arbitrary",) * n_axes,
                                vmem_limit_bytes=vmem_mib * MIB)


def _log_sigmoid(z):
    return jnp.minimum(z, 0.0) - jnp.log1p(jnp.exp(-jnp.abs(z)))


def _sigmoid(z):
    return 1.0 / (1.0 + jnp.exp(-z))


def _ada_kernel(c_ref, w_ref, b_ref, o_ref):
    c = c_ref[...]
    ca = c * _sigmoid(c)
    o_ref[...] = jnp.sum(ca * w_ref[...], axis=0, keepdims=True) + b_ref[...]


def _ada_mod(c_col, w, b_row):
    d, n = w.shape
    return pl.pallas_call(
        _ada_kernel,
        grid=(n // ADA_TN,),
        in_specs=[pl.BlockSpec((d, 1), lambda j: (0, 0)),
                  pl.BlockSpec((d, ADA_TN), lambda j: (0, j)),
                  pl.BlockSpec((1, ADA_TN), lambda j: (0, j))],
        out_specs=pl.BlockSpec((1, ADA_TN), lambda j: (0, j)),
        out_shape=jax.ShapeDtypeStruct((1, n), F32),
        compiler_params=_params(1, 32),
        name="ada_mod",
    )(c_col, w, b_row)


_FQ_COL = 2 * MLSTM_HEADS * MLSTM_DQK + 2 * MLSTM_WIDTH
_NORM_TILE_LO = _FQ_COL // INPROJ_TN
_NORM_TILE_HI = (_FQ_COL + 2 * FOX_WIDTH) // INPROJ_TN


def _inproj_kernel(x_ref, g_ref, sh_ref, sc_ref, w_ref, wg_ref, cg_ref, p_ref, gate_ref, hn_ref):
    j = pl.program_id(1)

    @pl.when(j == 0)
    def _():
        x = x_ref[...]
        ms = jnp.mean(x * x, axis=-1, keepdims=True)
        hn = x * lax.rsqrt(ms + EPS) * g_ref[...] * (1.0 + sc_ref[...]) + sh_ref[...]
        hnb = hn.astype(BF16)
        hn_ref[...] = hnb
        gate_ref[...] = jnp.dot(hnb, wg_ref[...], preferred_element_type=F32)

    acc = jnp.dot(hn_ref[...], w_ref[...], preferred_element_type=F32)
    is_norm = jnp.logical_and(j >= _NORM_TILE_LO, j < _NORM_TILE_HI)

    @pl.when(is_norm)
    def _():
        parts = []
        for hh in range(INPROJ_TN // FOX_HEAD_DIM):
            a = acc[:, hh * FOX_HEAD_DIM:(hh + 1) * FOX_HEAD_DIM]
            parts.append(a * lax.rsqrt(jnp.mean(a * a, axis=-1, keepdims=True) + EPS))
        p_ref[...] = (jnp.concatenate(parts, axis=-1) * cg_ref[...]).astype(BF16)

    @pl.when(jnp.logical_not(is_norm))
    def _():
        p_ref[...] = (acc * cg_ref[...]).astype(BF16)


def _in_proj(x2, g_row, sh_row, sc_row, wp, wg, colgain):
    s, d = x2.shape
    tm, tn = INPROJ_TM, INPROJ_TN
    return pl.pallas_call(
        _inproj_kernel,
        grid=(s // tm, P_WIDTH // tn),
        in_specs=[pl.BlockSpec((tm, d), lambda i, j: (i, 0)),
                  pl.BlockSpec((1, d), lambda i, j: (0, 0)),
                  pl.BlockSpec((1, d), lambda i, j: (0, 0)),
                  pl.BlockSpec((1, d), lambda i, j: (0, 0)),
                  pl.BlockSpec((d, tn), lambda i, j: (0, j)),
                  pl.BlockSpec((d, LANES), lambda i, j: (0, 0)),
                  pl.BlockSpec((1, tn), lambda i, j: (0, j))],
        out_specs=[pl.BlockSpec((tm, tn), lambda i, j: (i, j)),
                   pl.BlockSpec((tm, LANES), lambda i, j: (i, 0))],
        out_shape=[jax.ShapeDtypeStruct((s, P_WIDTH), BF16),
                   jax.ShapeDtypeStruct((s, LANES), F32)],
        scratch_shapes=[pltpu.VMEM((tm, d), BF16)],
        compiler_params=_params(2, 48),
        name="in_proj",
    )(x2, g_row, sh_row, sc_row, wp, wg, colgain)


def _gates_kernel(g_ref, b_ref, a_ref, carry_ref):
    i = pl.program_id(0)

    @pl.when(i == 0)
    def _():
        carry_ref[...] = jnp.zeros_like(carry_ref)

    tl = g_ref.shape[0]
    z = g_ref[...] + b_ref[...]
    col = lax.broadcasted_iota(I32, z.shape, 1)
    capped = GATE_SOFTCAP * jnp.tanh(z / GATE_SOFTCAP)
    r = lax.broadcasted_iota(I32, (tl, tl), 0)
    cc = lax.broadcasted_iota(I32, (tl, tl), 1)
    tril = (cc <= r).astype(F32)
    cs = jnp.dot(tril, _log_sigmoid(z), precision=HIGHEST, preferred_element_type=F32) + carry_ref[...]
    carry_ref[...] = cs[tl - 1:tl, :]
    a_ref[...] = jnp.where(col < MLSTM_HEADS, capped,
                           jnp.where(col < 2 * MLSTM_HEADS, _log_sigmoid(capped), cs))


def _gates(gates, bias_row):
    s = gates.shape[0]
    tl = GATES_TL
    return pl.pallas_call(
        _gates_kernel,
        grid=(s // tl,),
        in_specs=[pl.BlockSpec((tl, LANES), lambda i: (i, 0)),
                  pl.BlockSpec((1, LANES), lambda i: (0, 0))],
        out_specs=pl.BlockSpec((tl, LANES), lambda i: (i, 0)),
        out_shape=jax.ShapeDtypeStruct((s, LANES), F32),
        scratch_shapes=[pltpu.VMEM((1, LANES), F32)],
        compiler_params=_params(1, 32),
        name="gates",
    )(gates, bias_row)


def _mlstm_kernel(q_ref, k_ref, v_ref, mo_ref, a_ref, at_ref, gain_ref, o_ref, c_ref, n_ref, m_ref):
    ci = pl.program_id(0)
    L = MLSTM_L

    @pl.when(ci == 0)
    def _():
        c_ref[...] = jnp.zeros_like(c_ref)
        n_ref[...] = jnp.zeros_like(n_ref)
        m_ref[...] = jnp.zeros_like(m_ref)

    r = lax.broadcasted_iota(I32, (L, L), 0)
    cc = lax.broadcasted_iota(I32, (L, L), 1)
    causal = cc <= r
    tril = causal.astype(F32)
    a = a_ref[...]
    at = at_ref[...]
    g_cols = jnp.dot(tril, a, precision=HIGHEST, preferred_element_type=F32)
    g_rows = jnp.dot(at, (r <= cc).astype(F32), precision=HIGHEST, preferred_element_type=F32)

    for hh in range(MLSTM_HEADS):
        q = q_ref[:, hh * MLSTM_DQK:(hh + 1) * MLSTM_DQK]
        k = k_ref[:, hh * MLSTM_DQK:(hh + 1) * MLSTM_DQK]
        v = v_ref[:, hh * MLSTM_DV:(hh + 1) * MLSTM_DV]
        i_col = a[:, hh:hh + 1]
        i_row = at[hh:hh + 1, :]
        g_col = g_cols[:, MLSTM_HEADS + hh:MLSTM_HEADS + hh + 1]
        g_row = g_rows[MLSTM_HEADS + hh:MLSTM_HEADS + hh + 1, :]
        m_prev = m_ref[hh][:, 0:1]
        c_prev = c_ref[hh]
        n_prev = n_ref[hh]

        m_inter = g_col + m_prev
        dmat = jnp.where(causal, g_col - g_row + i_row, NEG_INF)
        m_t = jnp.maximum(m_inter, jnp.max(dmat, axis=-1, keepdims=True))
        qk = lax.dot_general(q, k, (((1,), (1,)), ((), ())), preferred_element_type=F32)
        scores = qk * jnp.exp(dmat - m_t)
        inter = jnp.exp(m_inter - m_t)
        q_c = jnp.dot(q, c_prev.astype(BF16), preferred_element_type=F32)
        num = jnp.dot(scores.astype(BF16), v, preferred_element_type=F32) + inter * q_c
        q_n = jnp.sum(q.astype(F32) * n_prev, axis=-1, keepdims=True)
        den = jnp.sum(scores, axis=-1, keepdims=True) + inter * q_n
        h_out = num / jnp.maximum(jnp.abs(den), jnp.exp(-m_t))

        g_last = g_col[L - 1:L, :]
        log_w = g_last - g_col + i_col
        m_new = jnp.maximum(g_last + m_prev, jnp.max(log_w, axis=0, keepdims=True))
        w_col = jnp.exp(log_w - m_new)
        decay = jnp.exp(g_last + m_prev - m_new)
        kw = k.astype(F32) * w_col
        upd = jnp.dot(kw.T.astype(BF16), v, preferred_element_type=F32)
        c_ref[hh] = decay * c_prev + upd
        n_ref[hh] = decay * n_prev + jnp.sum(kw, axis=0, keepdims=True)
        m_ref[hh] = jnp.broadcast_to(m_new, (1, LANES))

        hn = h_out * lax.rsqrt(jnp.mean(h_out * h_out, axis=-1, keepdims=True) + EPS)
        hn = hn * gain_ref[:, hh * MLSTM_DV:(hh + 1) * MLSTM_DV]
        mo = mo_ref[:, hh * MLSTM_DV:(hh + 1) * MLSTM_DV].astype(F32)
        o_ref[:, hh * MLSTM_DV:(hh + 1) * MLSTM_DV] = (hn * _sigmoid(mo)).astype(BF16)


def _mlstm(p, a, at, gain_row):
    s = p.shape[0]
    L = MLSTM_L
    qw = MLSTM_HEADS * MLSTM_DQK
    return pl.pallas_call(
        _mlstm_kernel,
        grid=(s // L,),
        in_specs=[pl.BlockSpec((L, qw), lambda i: (i, 0)),
                  pl.BlockSpec((L, qw), lambda i: (i, 1)),
                  pl.BlockSpec((L, MLSTM_WIDTH), lambda i: (i, 1)),
                  pl.BlockSpec((L, MLSTM_WIDTH), lambda i: (i, 2)),
                  pl.BlockSpec((L, LANES), lambda i: (i, 0)),
                  pl.BlockSpec((16, L), lambda i: (0, i)),
                  pl.BlockSpec((1, MLSTM_WIDTH), lambda i: (0, 0))],
        out_specs=pl.BlockSpec((L, MLSTM_WIDTH), lambda i: (i, 0)),
        out_shape=jax.ShapeDtypeStruct((s, MLSTM_WIDTH), BF16),
        scratch_shapes=[pltpu.VMEM((MLSTM_HEADS, MLSTM_DQK, MLSTM_DV), F32),
                        pltpu.VMEM((MLSTM_HEADS, 1, MLSTM_DQK), F32),
                        pltpu.VMEM((MLSTM_HEADS, 1, LANES), F32)],
        compiler_params=_params(1, 32),
        name="mlstm",
    )(p, p, p, p, a, at, gain_row)


def _fox_kernel(q_ref, k_ref, v_ref, frow_ref, fref_ref, gain_ref, o_ref):
    qi = pl.program_id(1)
    bq = FOX_BQ
    q = q_ref[...]
    fref = fref_ref[0, 0]

    def step(ki, carry, masked):
        m, l, acc = carry
        off = pl.multiple_of(ki * bq, bq)
        k = k_ref[pl.ds(off, bq), :]
        v = v_ref[pl.ds(off, bq), :]
        s = lax.dot_general(q, k, (((1,), (1,)), ((), ())), preferred_element_type=F32)
        s = s + (fref - frow_ref[0, :, pl.ds(off, bq)])
        if masked:
            r = lax.broadcasted_iota(I32, (bq, bq), 0)
            c = lax.broadcasted_iota(I32, (bq, bq), 1)
            s = jnp.where(c <= r, s, NEG_INF)
        m_new = jnp.maximum(m, jnp.max(s, axis=-1, keepdims=True))
        alpha = jnp.exp(m - m_new)
        p = jnp.exp(s - m_new)
        l = alpha * l + jnp.sum(p, axis=-1, keepdims=True)
        acc = alpha * acc + jnp.dot(p.astype(BF16), v, preferred_element_type=F32)
        return m_new, l, acc

    init = (jnp.full((bq, 1), NEG_INF, F32), jnp.zeros((bq, 1), F32), jnp.zeros((bq, FOX_HEAD_DIM), F32))
    carry = lax.fori_loop(0, qi, lambda ki, c: step(ki, c, False), init)
    _, l, acc = step(qi, carry, True)
    out = acc / l
    out = out * lax.rsqrt(jnp.mean(out * out, axis=-1, keepdims=True) + EPS) * gain_ref[0]
    o_ref[...] = out.astype(BF16)


def _fox_attn(p, frow3, fref4, gain3):
    s = p.shape[0]
    bq = FOX_BQ
    qcol = 3072 // FOX_HEAD_DIM
    kcol = 4096 // FOX_HEAD_DIM
    vcol = 5120 // FOX_HEAD_DIM
    return pl.pallas_call(
        _fox_kernel,
        grid=(FOX_HEADS, s // bq),
        in_specs=[pl.BlockSpec((bq, FOX_HEAD_DIM), lambda h, i: (i, qcol + h)),
                  pl.BlockSpec((s, FOX_HEAD_DIM), lambda h, i: (0, kcol + h)),
                  pl.BlockSpec((s, FOX_HEAD_DIM), lambda h, i: (0, vcol + h)),
                  pl.BlockSpec((1, 1, s), lambda h, i: (h, 0, 0)),
                  pl.BlockSpec((1, 1, 1, bq), lambda h, i: (h, i, 0, 0)),
                  pl.BlockSpec((1, 1, FOX_HEAD_DIM), lambda h, i: (h, 0, 0))],
        out_specs=pl.BlockSpec((bq, FOX_HEAD_DIM), lambda h, i: (i, h)),
        out_shape=jax.ShapeDtypeStruct((s, FOX_WIDTH), BF16),
        compiler_params=_params(2, 40),
        name="fox_attn",
    )(p, p, p, frow3, fref4, gain3)


def _outrouter_kernel(hm_ref, hf_ref, wt_ref, wb_ref, x_ref, g1_ref, nf_ref, sh2_ref, sc2_ref, wr_ref, br_ref,
                      hres_ref, h2a_ref, h2b_ref, route_ref, cnt_ref, carry_ref):
    i = pl.program_id(0)
    tm = x_ref.shape[0]

    @pl.when(i == 0)
    def _():
        carry_ref[...] = jnp.zeros_like(carry_ref)

    y = (jnp.dot(hm_ref[...], wt_ref[...], preferred_element_type=F32)
         + jnp.dot(hf_ref[...], wb_ref[...], preferred_element_type=F32))
    hres = x_ref[...] + g1_ref[...] * y
    hres_ref[...] = hres
    ms = jnp.mean(hres * hres, axis=-1, keepdims=True)
    h2 = hres * lax.rsqrt(ms + EPS) * nf_ref[...] * (1.0 + sc2_ref[...]) + sh2_ref[...]

    h2a_ref[...] = lax.bitcast_convert_type(h2[:, :HALF], U32)
    h2b_ref[...] = lax.bitcast_convert_type(h2[:, HALF:], U32)

    logits = jnp.dot(h2, wr_ref[...], precision=HIGHEST, preferred_element_type=F32) + br_ref[...]
    col = lax.broadcasted_iota(I32, logits.shape, 1)
    colf = col.astype(F32)
    lg = jnp.where(col < N_EXPERTS, logits, NEG_INF)
    vals, idxs = [], []
    for _ in range(TOP_K):
        mx = jnp.max(lg, axis=-1, keepdims=True)
        idx = jnp.min(jnp.where(lg == mx, colf, float(LANES)), axis=-1, keepdims=True)
        vals.append(mx)
        idxs.append(idx)
        lg = jnp.where(colf == idx, NEG_INF, lg)
    exps = [jnp.exp(vv - vals[0]) for vv in vals]
    denom = exps[0] + exps[1] + exps[2] + exps[3]
    sel = jnp.zeros(logits.shape, F32)
    for idx in idxs:
        sel = sel + (colf == idx).astype(F32)

    r = lax.broadcasted_iota(I32, (tm, tm), 0)
    cc = lax.broadcasted_iota(I32, (tm, tm), 1)
    tril = (cc <= r).astype(BF16)
    incl = jnp.dot(tril, sel.astype(BF16), preferred_element_type=F32) + carry_ref[...]
    carry_ref[...] = incl[tm - 1:tm, :]
    cnt_ref[...] = incl[tm - 1:tm, :]
    excl = incl - sel

    route = jnp.zeros(logits.shape, F32)
    for kk in range(TOP_K):
        rank = jnp.sum(jnp.where(colf == idxs[kk], excl, 0.0), axis=-1, keepdims=True)
        route = route + jnp.where(col == kk, idxs[kk], 0.0)
        route = route + jnp.where(col == TOP_K + kk, rank, 0.0)
        route = route + jnp.where(col == 2 * TOP_K + kk, exps[kk] / denom, 0.0)
    route_ref[...] = route


def _out_router(hm, hf, wt, wb, x2, g1, nf, sh2, sc2, wr, br):
    s, d = x2.shape
    tm = OUT_TM
    row = lambda i: (0, 0)
    return pl.pallas_call(
        _outrouter_kernel,
        grid=(s // tm,),
        in_specs=[pl.BlockSpec((tm, MLSTM_WIDTH), lambda i: (i, 0)),
                  pl.BlockSpec((tm, FOX_WIDTH), lambda i: (i, 0)),
                  pl.BlockSpec((MLSTM_WIDTH, d), row),
                  pl.BlockSpec((FOX_WIDTH, d), row),
                  pl.BlockSpec((tm, d), lambda i: (i, 0)),
                  pl.BlockSpec((1, d), row), pl.BlockSpec((1, d), row),
                  pl.BlockSpec((1, d), row), pl.BlockSpec((1, d), row),
                  pl.BlockSpec((d, LANES), row), pl.BlockSpec((1, LANES), row)],
        out_specs=[pl.BlockSpec((tm, d), lambda i: (i, 0)),
                   pl.BlockSpec((tm, HALF), lambda i: (i, 0)),
                   pl.BlockSpec((tm, HALF), lambda i: (i, 0)),
                   pl.BlockSpec((tm, LANES), lambda i: (i, 0)),
                   pl.BlockSpec((1, LANES), row)],
        out_shape=[jax.ShapeDtypeStruct((s, d), F32),
                   jax.ShapeDtypeStruct((s, HALF), U32),
                   jax.ShapeDtypeStruct((s, HALF), U32),
                   jax.ShapeDtypeStruct((s, LANES), F32),
                   jax.ShapeDtypeStruct((1, LANES), F32)],
        scratch_shapes=[pltpu.VMEM((1, LANES), F32)],
        compiler_params=_params(1, 48),
        name="out_router",
    )(hm, hf, wt, wb, x2, g1, nf, sh2, sc2, wr, br)


def _dispatch_kernel(zstart_ref, zflag_ref, dest_ref, h2a_hbm, h2b_hbm, xsa_hbm, xsb_hbm, zbuf, zsem, sem):
    i = pl.program_id(0)
    n = PAIRS_PER_STEP
    halves = ((h2a_hbm, xsa_hbm), (h2b_hbm, xsb_hbm))

    @pl.when(i == 0)
    def _():
        zbuf[...] = jnp.zeros_like(zbuf)

        def zero_copy(e, dst_hbm):
            z0 = pl.multiple_of(zstart_ref[e], ZERO_ROWS)
            return pltpu.make_async_copy(zbuf, dst_hbm.at[pl.ds(z0, ZERO_ROWS)], zsem)

        for e in range(N_EXPERTS):
            @pl.when(zflag_ref[e] == 1)
            def _():
                for _, dst_hbm in halves:
                    zero_copy(e, dst_hbm).start()
        for e in range(N_EXPERTS):
            @pl.when(zflag_ref[e] == 1)
            def _():
                for _, dst_hbm in halves:
                    zero_copy(e, dst_hbm).wait()

    def issue(j, _):
        tok = lax.shift_right_logical(i * n + j, 2)
        row = dest_ref[0, 0, j]
        for src_hbm, dst_hbm in halves:
            pltpu.make_async_copy(src_hbm.at[pl.ds(tok, 1)], dst_hbm.at[pl.ds(row, 1)], sem).start()
        return 0

    lax.fori_loop(0, n, issue, 0, unroll=8)
    for src_hbm, dst_hbm in halves:
        pltpu.make_async_copy(src_hbm.at[pl.ds(0, n)], dst_hbm.at[pl.ds(0, n)], sem).wait()


def _dispatch(zstart, zflag, dest3, h2a, h2b, n_rows):
    steps = dest3.shape[0]
    any_spec = pl.BlockSpec(memory_space=pl.ANY)
    grid_spec = pltpu.PrefetchScalarGridSpec(
        num_scalar_prefetch=2,
        grid=(steps,),
        in_specs=[pl.BlockSpec((1, 1, PAIRS_PER_STEP), lambda i, zs, zf: (i, 0, 0), memory_space=pltpu.SMEM),
                  any_spec, any_spec],
        out_specs=[any_spec, any_spec],
        scratch_shapes=[pltpu.VMEM((ZERO_ROWS, HALF), U32),
                        pltpu.SemaphoreType.DMA(()),
                        pltpu.SemaphoreType.DMA(())],
    )
    return pl.pallas_call(
        _dispatch_kernel,
        grid_spec=grid_spec,
        out_shape=[jax.ShapeDtypeStruct((n_rows, HALF), U32)] * 2,
        compiler_params=_params(1, 16),
        name="dispatch",
    )(zstart, zflag, dest3, h2a, h2b)


def _expert_kernel(ce_ref, cn_ref, cmap_ref, nact_ref,
                   xa_ref, xb_ref, wg_ref, wu_ref, wd_ref, bg_ref, bu_ref, bd_ref, o_ref,
                   wgb_ref, wub_ref, wdb_ref):
    c = pl.program_id(0)
    f = pl.program_id(1)

    @pl.when(c < nact_ref[0])
    def _():
        @pl.when(f == 0)
        def _():
            o_ref[...] = jnp.broadcast_to(bd_ref[...], o_ref.shape)

        wgb_ref[...] = wg_ref[...].astype(BF16)
        wub_ref[...] = wu_ref[...].astype(BF16)
        wdb_ref[...] = wd_ref[...].astype(BF16)
        bg = bg_ref[...]
        bu = bu_ref[...]

        def sub_block(r0, rows):
            xa = lax.bitcast_convert_type(xa_ref[pl.ds(r0, rows), :], F32).astype(BF16)
            xb = lax.bitcast_convert_type(xb_ref[pl.ds(r0, rows), :], F32).astype(BF16)
            gt = (jnp.dot(xa, wgb_ref[:HALF, :], preferred_element_type=F32)
                  + jnp.dot(xb, wgb_ref[HALF:, :], preferred_element_type=F32) + bg)
            up = (jnp.dot(xa, wub_ref[:HALF, :], preferred_element_type=F32)
                  + jnp.dot(xb, wub_ref[HALF:, :], preferred_element_type=F32) + bu)
            gate = jnp.minimum(gt, SWIGLU_LIMIT)
            up = jnp.clip(up, -SWIGLU_LIMIT, SWIGLU_LIMIT)
            act = (up + 1.0) * gate * _sigmoid(SWIGLU_ALPHA * gate)
            o_ref[pl.ds(r0, rows), :] += jnp.dot(act.astype(BF16), wdb_ref[...], preferred_element_type=F32)

        n = cn_ref[c]
        shift = SUB_ROWS.bit_length() - 1
        n_full = lax.shift_right_logical(n, shift)
        rem = n & (SUB_ROWS - 1)
        n_big = n_full + (rem > SUB_ROWS // 2).astype(I32)

        def pair(p, _):
            r0 = pl.multiple_of(p * (2 * SUB_ROWS), 2 * SUB_ROWS)
            sub_block(r0, SUB_ROWS)
            sub_block(r0 + SUB_ROWS, SUB_ROWS)
            return 0

        lax.fori_loop(0, lax.shift_right_logical(n_big, 1), pair, 0)

        @pl.when((n_big & 1) == 1)
        def _():
            sub_block(pl.multiple_of((n_big - 1) * SUB_ROWS, SUB_ROWS), SUB_ROWS)

        @pl.when(jnp.logical_and(rem > 0, rem <= SUB_ROWS // 2))
        def _():
            sub_block(pl.multiple_of(n_full * SUB_ROWS, SUB_ROWS), SUB_ROWS // 2)


def _experts(ce, cn, cmap, nact, xsa, xsb, w_up_gate, w_down, b_up_gate3, b_down3, n_chunks):
    rc, tf = ROWS_PER_CHUNK, FF_TILE
    nf = D_FF // tf
    d = D_MODEL

    def fsel(c, f, nact_ref):
        return jnp.where(c < nact_ref[0], f, nf - 1)

    grid_spec = pltpu.PrefetchScalarGridSpec(
        num_scalar_prefetch=4,
        grid=(n_chunks, nf),
        in_specs=[
            pl.BlockSpec((rc, HALF), lambda c, f, ce, cn, cm, na: (cm[c], 0)),
            pl.BlockSpec((rc, HALF), lambda c, f, ce, cn, cm, na: (cm[c], 0)),
            pl.BlockSpec((None, d, tf), lambda c, f, ce, cn, cm, na: (ce[c], 0, fsel(c, f, na))),
            pl.BlockSpec((None, d, tf), lambda c, f, ce, cn, cm, na: (ce[c], 0, nf + fsel(c, f, na))),
            pl.BlockSpec((None, tf, d), lambda c, f, ce, cn, cm, na: (ce[c], fsel(c, f, na), 0)),
            pl.BlockSpec((None, 1, tf), lambda c, f, ce, cn, cm, na: (ce[c], 0, fsel(c, f, na))),
            pl.BlockSpec((None, 1, tf), lambda c, f, ce, cn, cm, na: (ce[c], 0, nf + fsel(c, f, na))),
            pl.BlockSpec((None, 1, d), lambda c, f, ce, cn, cm, na: (ce[c], 0, 0)),
        ],
        out_specs=pl.BlockSpec((rc, d), lambda c, f, ce, cn, cm, na: (cm[c], 0)),
        scratch_shapes=[pltpu.VMEM((d, tf), BF16), pltpu.VMEM((d, tf), BF16), pltpu.VMEM((tf, d), BF16)],
    )
    return pl.pallas_call(
        _expert_kernel,
        grid_spec=grid_spec,
        out_shape=jax.ShapeDtypeStruct((n_chunks * rc, d), F32),
        compiler_params=_params(2, 58),
        name="experts",
    )(ce, cn, cmap, nact, xsa, xsb, w_up_gate, w_up_gate, w_down, b_up_gate3, b_up_gate3, b_down3)


def _final_kernel(dcur_ref, dnext_ref, ys_hbm, hres_ref, route_ref, g2_ref, nf_ref, shf_ref, scf_ref, o_ref,
                  buf, sems):
    i = pl.program_id(0)
    tm = FINAL_TM
    slot = i & 1

    def issue(d_ref, dst_slot):
        def body(j, _):
            tok = lax.shift_right_logical(j, 2)
            k = j & (TOP_K - 1)
            pltpu.make_async_copy(ys_hbm.at[pl.ds(d_ref[0, 0, j], 1)],
                                  buf.at[dst_slot, k, pl.ds(tok, 1)], sems.at[dst_slot]).start()
            return 0
        lax.fori_loop(0, TOP_K * tm, body, 0, unroll=8)

    @pl.when(i == 0)
    def _():
        issue(dcur_ref, 0)

    @pl.when(i + 1 < pl.num_programs(0))
    def _():
        issue(dnext_ref, 1 - slot)

    for k in range(TOP_K):
        pltpu.make_async_copy(ys_hbm.at[pl.ds(0, tm)], buf.at[slot, k], sems.at[slot]).wait()

    route = route_ref[...]
    ffn = buf[slot, 0] * route[:, 2 * TOP_K:2 * TOP_K + 1]
    for k in range(1, TOP_K):
        ffn = ffn + buf[slot, k] * route[:, 2 * TOP_K + k:2 * TOP_K + k + 1]
    h = hres_ref[...] + g2_ref[...] * ffn
    ms = jnp.mean(h * h, axis=-1, keepdims=True)
    o_ref[...] = h * lax.rsqrt(ms + EPS) * nf_ref[...] * (1.0 + scf_ref[...]) + shf_ref[...]


def _final(dest3, ys, hres, route, g2, nf, shf, scf):
    s, d = hres.shape
    tm = FINAL_TM
    steps = s // tm
    row = lambda i: (0, 0)
    return pl.pallas_call(
        _final_kernel,
        grid=(steps,),
        in_specs=[pl.BlockSpec((1, 1, TOP_K * tm), lambda i: (i, 0, 0), memory_space=pltpu.SMEM),
                  pl.BlockSpec((1, 1, TOP_K * tm), lambda i: (jnp.minimum(i + 1, steps - 1), 0, 0),
                               memory_space=pltpu.SMEM),
                  pl.BlockSpec(memory_space=pl.ANY),
                  pl.BlockSpec((tm, d), lambda i: (i, 0)),
                  pl.BlockSpec((tm, LANES), lambda i: (i, 0)),
                  pl.BlockSpec((1, d), row), pl.BlockSpec((1, d), row),
                  pl.BlockSpec((1, d), row), pl.BlockSpec((1, d), row)],
        out_specs=pl.BlockSpec((tm, d), lambda i: (i, 0)),
        out_shape=jax.ShapeDtypeStruct((s, d), F32),
        scratch_shapes=[pltpu.VMEM((2, TOP_K, tm, d), F32), pltpu.SemaphoreType.DMA((2,))],
        compiler_params=_params(1, 40),
        name="final",
    )(dest3, dest3, ys, hres, route, g2, nf, shf, scf)


def kernel(x, c, w_ada, b_ada, norm_mix, w_in, b_i, b_f, fox_b_f, fox_q_norm, fox_k_norm, mlstm_out_norm,
           fox_out_norm, w_out, norm_ffn, w_router, b_router, w_up_gate, b_up_gate, w_down, b_down,
           w_ada_final, b_ada_final, norm_final):
    b, s, d = x.shape
    assert b == 1 and d == D_MODEL and w_ada.shape[0] == 1
    x2 = x.reshape(s, d).astype(F32)
    c_col = c.astype(F32).reshape(d, 1)

    mod = _ada_mod(c_col, w_ada[0], b_ada[0].reshape(1, -1))
    sh1, sc1, g1, sh2, sc2, g2 = [mod[:, i * d:(i + 1) * d] for i in range(6)]
    modf = _ada_mod(c_col, w_ada_final, b_ada_final.reshape(1, -1))
    shf, scf = modf[:, :d], modf[:, d:]

    w = w_in[0]
    o_gate = 2 * MLSTM_HEADS * MLSTM_DQK + 2 * MLSTM_WIDTH
    o_fq = o_gate + 2 * MLSTM_HEADS
    o_ff = o_fq + 3 * FOX_WIDTH
    wp = jnp.concatenate([w[:, :o_gate], w[:, o_fq:o_ff]], axis=1).astype(BF16)
    wg = jnp.concatenate([w[:, o_gate:o_fq], w[:, o_ff:o_ff + FOX_HEADS],
                          jnp.zeros((d, LANES - 16), F32)], axis=1).astype(BF16)
    qscale = MLSTM_DQK ** -0.5
    fscale = FOX_HEAD_DIM ** -0.5
    ones = lambda n: jnp.ones((n,), F32)
    colgain = jnp.concatenate([
        ones(MLSTM_HEADS * MLSTM_DQK) * qscale, ones(MLSTM_HEADS * MLSTM_DQK), ones(2 * MLSTM_WIDTH),
        jnp.tile(fox_q_norm[0].astype(F32), FOX_HEADS) * fscale, jnp.tile(fox_k_norm[0].astype(F32), FOX_HEADS),
        ones(FOX_WIDTH)]).reshape(1, P_WIDTH)

    p, gates = _in_proj(x2, norm_mix[0].reshape(1, d), sh1, sc1, wp, wg, colgain)

    gate_bias = jnp.concatenate([b_i[0], b_f[0], fox_b_f[0], jnp.zeros((LANES - 16,), F32)]).reshape(1, LANES)
    a = _gates(gates, gate_bias)
    at = a[:, :16].T

    hm = _mlstm(p, a, at, mlstm_out_norm[0].reshape(1, MLSTM_WIDTH).astype(F32))

    nq = s // FOX_BQ
    frow = at[2 * MLSTM_HEADS:2 * MLSTM_HEADS + FOX_HEADS]
    fref4 = jnp.broadcast_to(frow[:, ::FOX_BQ][:, :, None, None], (FOX_HEADS, nq, 1, FOX_BQ))
    hf = _fox_attn(p, frow.reshape(FOX_HEADS, 1, s), fref4,
                   fox_out_norm[0].reshape(FOX_HEADS, 1, FOX_HEAD_DIM).astype(F32))

    wo = w_out[0].astype(BF16)
    wr = jnp.concatenate([w_router[0].astype(F32), jnp.zeros((d, LANES - N_EXPERTS), F32)], axis=1)
    br = jnp.concatenate([b_router[0].astype(F32), jnp.zeros((LANES - N_EXPERTS,), F32)]).reshape(1, LANES)
    hres, h2a, h2b, route, cnt = _out_router(hm, hf, wo[:MLSTM_WIDTH], wo[MLSTM_WIDTH:], x2, g1,
                                        norm_ffn[0].reshape(1, d), sh2, sc2, wr, br)

    rc = ROWS_PER_CHUNK
    n_chunks = N_EXPERTS + (s * TOP_K) // rc
    counts = cnt[0, :N_EXPERTS].astype(I32)
    eidx = route[:, :TOP_K].astype(I32)
    rank = route[:, TOP_K:2 * TOP_K].astype(I32)
    nch = (counts + rc - 1) // rc
    cend = jnp.cumsum(nch)
    cstart = cend - nch
    nact = cend[-1]
    dest = cstart[eidx] * rc + rank
    cidx = jnp.arange(n_chunks, dtype=I32)
    cmap = jnp.minimum(cidx, nact - 1)
    ce = jnp.minimum(jnp.sum((cmap[:, None] >= cend[None, :]).astype(I32), axis=1), N_EXPERTS - 1)
    cn = jnp.where(cidx < nact, jnp.clip(counts[ce] - (cmap - cstart[ce]) * rc, 0, rc), 0).astype(I32)
    zflag = (counts % ZERO_ROWS != 0).astype(I32)
    zstart = (cstart * rc + (counts // ZERO_ROWS) * ZERO_ROWS).astype(I32)

    xsa, xsb = _dispatch(zstart, zflag, dest.reshape(-1, 1, PAIRS_PER_STEP), h2a, h2b, n_chunks * rc)
    ys = _experts(ce, cn, cmap, nact.reshape(1).astype(I32), xsa, xsb, w_up_gate[0], w_down[0],
                  b_up_gate[0].reshape(N_EXPERTS, 1, 2 * D_FF), b_down[0].reshape(N_EXPERTS, 1, d), n_chunks)
    out = _final(dest.reshape(-1, 1, TOP_K * FINAL_TM), ys, hres, route, g2, norm_final.reshape(1, d), shf, scf)
    return out.reshape(b, s, d).astype(x.dtype)
```

```python
import jax
import jax.numpy as jnp
from jax import lax
from jax.experimental import pallas as pl
from jax.experimental.pallas import tpu as pltpu

F32 = jnp.float32
BF16 = jnp.bfloat16
I32 = jnp.int32
U32 = jnp.uint32
NEG_INF = float("-inf")
HIGHEST = lax.Precision.HIGHEST

D_MODEL = 2048
MLSTM_HEADS = 4
MLSTM_DQK = 128
MLSTM_DV = 256
MLSTM_WIDTH = MLSTM_HEADS * MLSTM_DV
GATE_SOFTCAP = 15.0
FOX_HEADS = 8
FOX_HEAD_DIM = 128
FOX_WIDTH = FOX_HEADS * FOX_HEAD_DIM
N_EXPERTS = 32
TOP_K = 4
D_FF = 2048
SWIGLU_LIMIT = 7.0
SWIGLU_ALPHA = 1.702
EPS = 1e-6

LANES = 128
P_WIDTH = 6144

ADA_TN = 512
INPROJ_TM = 1024
INPROJ_TN = 512
GATES_TL = 512
MLSTM_L = 256
FOX_BQ = 512
OUT_TM = 256
ROWS_PER_CHUNK = 1152
SUB_ROWS = 256
ZERO_ROWS = SUB_ROWS // 2
HALF = D_MODEL // 2
FF_TILE = 256
PAIRS_PER_STEP = 2048
FINAL_TM = 256
MIB = 1024 * 1024


def _params(n_axes, vmem_mib):
    return pltpu.CompilerParams(dimension_semantics=("arbitrary",) * n_axes,
                                vmem_limit_bytes=vmem_mib * MIB)


def _log_sigmoid(z):
    return jnp.minimum(z, 0.0) - jnp.log1p(jnp.exp(-jnp.abs(z)))


def _sigmoid(z):
    return 1.0 / (1.0 + jnp.exp(-z))


def _ada_kernel(c_ref, w_ref, b_ref, o_ref):
    c = c_ref[...]
    ca = c * _sigmoid(c)
    o_ref[...] = jnp.sum(ca * w_ref[...], axis=0, keepdims=True) + b_ref[...]


def _ada_mod(c_col, w, b_row):
    d, n = w.shape
    return pl.pallas_call(
        _ada_kernel,
        grid=(n // ADA_TN,),
        in_specs=[pl.BlockSpec((d, 1), lambda j: (0, 0)),
                  pl.BlockSpec((d, ADA_TN), lambda j: (0, j)),
                  pl.BlockSpec((1, ADA_TN), lambda j: (0, j))],
        out_specs=pl.BlockSpec((1, ADA_TN), lambda j: (0, j)),
        out_shape=jax.ShapeDtypeStruct((1, n), F32),
        compiler_params=_params(1, 32),
        name="ada_mod",
    )(c_col, w, b_row)


_FQ_COL = 2 * MLSTM_HEADS * MLSTM_DQK + 2 * MLSTM_WIDTH
_NORM_TILE_LO = _FQ_COL // INPROJ_TN
_NORM_TILE_HI = (_FQ_COL + 2 * FOX_WIDTH) // INPROJ_TN


def _inproj_kernel(x_ref, g_ref, sh_ref, sc_ref, w_ref, wg_ref, cg_ref, p_ref, gate_ref, hn_ref):
    j = pl.program_id(1)

    @pl.when(j == 0)
    def _():
        x = x_ref[...]
        ms = jnp.mean(x * x, axis=-1, keepdims=True)
        hn = x * lax.rsqrt(ms + EPS) * g_ref[...] * (1.0 + sc_ref[...]) + sh_ref[...]
        hnb = hn.astype(BF16)
        hn_ref[...] = hnb
        gate_ref[...] = jnp.dot(hnb, wg_ref[...], preferred_element_type=F32)

    acc = jnp.dot(hn_ref[...], w_ref[...], preferred_element_type=F32)
    is_norm = jnp.logical_and(j >= _NORM_TILE_LO, j < _NORM_TILE_HI)

    @pl.when(is_norm)
    def _():
        parts = []
        for hh in range(INPROJ_TN // FOX_HEAD_DIM):
            a = acc[:, hh * FOX_HEAD_DIM:(hh + 1) * FOX_HEAD_DIM]
            parts.append(a * lax.rsqrt(jnp.mean(a * a, axis=-1, keepdims=True) + EPS))
        p_ref[...] = (jnp.concatenate(parts, axis=-1) * cg_ref[...]).astype(BF16)

    @pl.when(jnp.logical_not(is_norm))
    def _():
        p_ref[...] = (acc * cg_ref[...]).astype(BF16)


def _in_proj(x2, g_row, sh_row, sc_row, wp, wg, colgain):
    s, d = x2.shape
    tm, tn = INPROJ_TM, INPROJ_TN
    return pl.pallas_call(
        _inproj_kernel,
        grid=(s // tm, P_WIDTH // tn),
        in_specs=[pl.BlockSpec((tm, d), lambda i, j: (i, 0)),
                  pl.BlockSpec((1, d), lambda i, j: (0, 0)),
                  pl.BlockSpec((1, d), lambda i, j: (0, 0)),
                  pl.BlockSpec((1, d), lambda i, j: (0, 0)),
                  pl.BlockSpec((d, tn), lambda i, j: (0, j)),
                  pl.BlockSpec((d, LANES), lambda i, j: (0, 0)),
                  pl.BlockSpec((1, tn), lambda i, j: (0, j))],
        out_specs=[pl.BlockSpec((tm, tn), lambda i, j: (i, j)),
                   pl.BlockSpec((tm, LANES), lambda i, j: (i, 0))],
        out_shape=[jax.ShapeDtypeStruct((s, P_WIDTH), BF16),
                   jax.ShapeDtypeStruct((s, LANES), F32)],
        scratch_shapes=[pltpu.VMEM((tm, d), BF16)],
        compiler_params=_params(2, 48),
        name="in_proj",
    )(x2, g_row, sh_row, sc_row, wp, wg, colgain)


def _gates_kernel(g_ref, b_ref, a_ref, carry_ref):
    i = pl.program_id(0)

    @pl.when(i == 0)
    def _():
        carry_ref[...] = jnp.zeros_like(carry_ref)

    tl = g_ref.shape[0]
    z = g_ref[...] + b_ref[...]
    col = lax.broadcasted_iota(I32, z.shape, 1)
    capped = GATE_SOFTCAP * jnp.tanh(z / GATE_SOFTCAP)
    r = lax.broadcasted_iota(I32, (tl, tl), 0)
    cc = lax.broadcasted_iota(I32, (tl, tl), 1)
    tril = (cc <= r).astype(F32)
    cs = jnp.dot(tril, _log_sigmoid(z), precision=HIGHEST, preferred_element_type=F32) + carry_ref[...]
    carry_ref[...] = cs[tl - 1:tl, :]
    a_ref[...] = jnp.where(col < MLSTM_HEADS, capped,
                           jnp.where(col < 2 * MLSTM_HEADS, _log_sigmoid(capped), cs))


def _gates(gates, bias_row):
    s = gates.shape[0]
    tl = GATES_TL
    return pl.pallas_call(
        _gates_kernel,
        grid=(s // tl,),
        in_specs=[pl.BlockSpec((tl, LANES), lambda i: (i, 0)),
                  pl.BlockSpec((1, LANES), lambda i: (0, 0))],
        out_specs=pl.BlockSpec((tl, LANES), lambda i: (i, 0)),
        out_shape=jax.ShapeDtypeStruct((s, LANES), F32),
        scratch_shapes=[pltpu.VMEM((1, LANES), F32)],
        compiler_params=_params(1, 32),
        name="gates",
    )(gates, bias_row)


def _mlstm_kernel(q_ref, k_ref, v_ref, mo_ref, a_ref, at_ref, gain_ref, o_ref, c_ref, n_ref, m_ref):
    ci = pl.program_id(0)
    L = MLSTM_L

    @pl.when(ci == 0)
    def _():
        c_ref[...] = jnp.zeros_like(c_ref)
        n_ref[...] = jnp.zeros_like(n_ref)
        m_ref[...] = jnp.zeros_like(m_ref)

    r = lax.broadcasted_iota(I32, (L, L), 0)
    cc = lax.broadcasted_iota(I32, (L, L), 1)
    causal = cc <= r
    tril = causal.astype(F32)
    a = a_ref[...]
    at = at_ref[...]
    g_cols = jnp.dot(tril, a, precision=HIGHEST, preferred_element_type=F32)
    g_rows = jnp.dot(at, (r <= cc).astype(F32), precision=HIGHEST, preferred_element_type=F32)

    for hh in range(MLSTM_HEADS):
        q = q_ref[:, hh * MLSTM_DQK:(hh + 1) * MLSTM_DQK]
        k = k_ref[:, hh * MLSTM_DQK:(hh + 1) * MLSTM_DQK]
        v = v_ref[:, hh * MLSTM_DV:(hh + 1) * MLSTM_DV]
        i_col = a[:, hh:hh + 1]
        i_row = at[hh:hh + 1, :]
        g_col = g_cols[:, MLSTM_HEADS + hh:MLSTM_HEADS + hh + 1]
        g_row = g_rows[MLSTM_HEADS + hh:MLSTM_HEADS + hh + 1, :]
        m_prev = m_ref[hh][:, 0:1]
        c_prev = c_ref[hh]
        n_prev = n_ref[hh]

        m_inter = g_col + m_prev
        dmat = jnp.where(causal, g_col - g_row + i_row, NEG_INF)
        m_t = jnp.maximum(m_inter, jnp.max(dmat, axis=-1, keepdims=True))
        qk = lax.dot_general(q, k, (((1,), (1,)), ((), ())), preferred_element_type=F32)
        scores = qk * jnp.exp(dmat - m_t)
        inter = jnp.exp(m_inter - m_t)
        q_c = jnp.dot(q, c_prev.astype(BF16), preferred_element_type=F32)
        num = jnp.dot(scores.astype(BF16), v, preferred_element_type=F32) + inter * q_c
        q_n = jnp.sum(q.astype(F32) * n_prev, axis=-1, keepdims=True)
        den = jnp.sum(scores, axis=-1, keepdims=True) + inter * q_n
        h_out = num / jnp.maximum(jnp.abs(den), jnp.exp(-m_t))

        g_last = g_col[L - 1:L, :]
        log_w = g_last - g_col + i_col
        m_new = jnp.maximum(g_last + m_prev, jnp.max(log_w, axis=0, keepdims=True))
        w_col = jnp.exp(log_w - m_new)
        decay = jnp.exp(g_last + m_prev - m_new)
        kw = k.astype(F32) * w_col
        upd = jnp.dot(kw.T.astype(BF16), v, preferred_element_type=F32)
        c_ref[hh] = decay * c_prev + upd
        n_ref[hh] = decay * n_prev + jnp.sum(kw, axis=0, keepdims=True)
        m_ref[hh] = jnp.broadcast_to(m_new, (1, LANES))

        hn = h_out * lax.rsqrt(jnp.mean(h_out * h_out, axis=-1, keepdims=True) + EPS)
        hn = hn * gain_ref[:, hh * MLSTM_DV:(hh + 1) * MLSTM_DV]
        mo = mo_ref[:, hh * MLSTM_DV:(hh + 1) * MLSTM_DV].astype(F32)
        o_ref[:, hh * MLSTM_DV:(hh + 1) * MLSTM_DV] = (hn * _sigmoid(mo)).astype(BF16)


def _mlstm(p, a, at, gain_row):
    s = p.shape[0]
    L = MLSTM_L
    qw = MLSTM_HEADS * MLSTM_DQK
    return pl.pallas_call(
        _mlstm_kernel,
        grid=(s // L,),
        in_specs=[pl.BlockSpec((L, qw), lambda i: (i, 0)),
                  pl.BlockSpec((L, qw), lambda i: (i, 1)),
                  pl.BlockSpec((L, MLSTM_WIDTH), lambda i: (i, 1)),
                  pl.BlockSpec((L, MLSTM_WIDTH), lambda i: (i, 2)),
                  pl.BlockSpec((L, LANES), lambda i: (i, 0)),
                  pl.BlockSpec((16, L), lambda i: (0, i)),
                  pl.BlockSpec((1, MLSTM_WIDTH), lambda i: (0, 0))],
        out_specs=pl.BlockSpec((L, MLSTM_WIDTH), lambda i: (i, 0)),
        out_shape=jax.ShapeDtypeStruct((s, MLSTM_WIDTH), BF16),
        scratch_shapes=[pltpu.VMEM((MLSTM_HEADS, MLSTM_DQK, MLSTM_DV), F32),
                        pltpu.VMEM((MLSTM_HEADS, 1, MLSTM_DQK), F32),
                        pltpu.VMEM((MLSTM_HEADS, 1, LANES), F32)],
        compiler_params=_params(1, 32),
        name="mlstm",
    )(p, p, p, p, a, at, gain_row)


def _fox_kernel(q_ref, k_ref, v_ref, frow_ref, fref_ref, gain_ref, o_ref):
    qi = pl.program_id(1)
    bq = FOX_BQ
    q = q_ref[...]
    fref = fref_ref[0, 0]

    def step(ki, carry, masked):
        m, l, acc = carry
        off = pl.multiple_of(ki * bq, bq)
        k = k_ref[pl.ds(off, bq), :]
        v = v_ref[pl.ds(off, bq), :]
        s = lax.dot_general(q, k, (((1,), (1,)), ((), ())), preferred_element_type=F32)
        s = s + (fref - frow_ref[0, :, pl.ds(off, bq)])
        if masked:
            r = lax.broadcasted_iota(I32, (bq, bq), 0)
            c = lax.broadcasted_iota(I32, (bq, bq), 1)
            s = jnp.where(c <= r, s, NEG_INF)
        m_new = jnp.maximum(m, jnp.max(s, axis=-1, keepdims=True))
        alpha = jnp.exp(m - m_new)
        p = jnp.exp(s - m_new)
        l = alpha * l + jnp.sum(p, axis=-1, keepdims=True)
        acc = alpha * acc + jnp.dot(p.astype(BF16), v, preferred_element_type=F32)
        return m_new, l, acc

    init = (jnp.full((bq, 1), NEG_INF, F32), jnp.zeros((bq, 1), F32), jnp.zeros((bq, FOX_HEAD_DIM), F32))
    carry = lax.fori_loop(0, qi, lambda ki, c: step(ki, c, False), init)
    _, l, acc = step(qi, carry, True)
    out = acc / l
    out = out * lax.rsqrt(jnp.mean(out * out, axis=-1, keepdims=True) + EPS) * gain_ref[0]
    o_ref[...] = out.astype(BF16)


def _fox_attn(p, frow3, fref4, gain3):
    s = p.shape[0]
    bq = FOX_BQ
    qcol = 3072 // FOX_HEAD_DIM
    kcol = 4096 // FOX_HEAD_DIM
    vcol = 5120 // FOX_HEAD_DIM
    return pl.pallas_call(
        _fox_kernel,
        grid=(FOX_HEADS, s // bq),
        in_specs=[pl.BlockSpec((bq, FOX_HEAD_DIM), lambda h, i: (i, qcol + h)),
                  pl.BlockSpec((s, FOX_HEAD_DIM), lambda h, i: (0, kcol + h)),
                  pl.BlockSpec((s, FOX_HEAD_DIM), lambda h, i: (0, vcol + h)),
                  pl.BlockSpec((1, 1, s), lambda h, i: (h, 0, 0)),
                  pl.BlockSpec((1, 1, 1, bq), lambda h, i: (h, i, 0, 0)),
                  pl.BlockSpec((1, 1, FOX_HEAD_DIM), lambda h, i: (h, 0, 0))],
        out_specs=pl.BlockSpec((bq, FOX_HEAD_DIM), lambda h, i: (i, h)),
        out_shape=jax.ShapeDtypeStruct((s, FOX_WIDTH), BF16),
        compiler_params=_params(2, 40),
        name="fox_attn",
    )(p, p, p, frow3, fref4, gain3)


def _outrouter_kernel(hm_ref, hf_ref, wt_ref, wb_ref, x_ref, g1_ref, nf_ref, sh2_ref, sc2_ref, wr_ref, br_ref,
                      hres_ref, h2a_ref, h2b_ref, route_ref, cnt_ref, carry_ref):
    i = pl.program_id(0)
    tm = x_ref.shape[0]

    @pl.when(i == 0)
    def _():
        carry_ref[...] = jnp.zeros_like(carry_ref)

    y = (jnp.dot(hm_ref[...], wt_ref[...], preferred_element_type=F32)
         + jnp.dot(hf_ref[...], wb_ref[...], preferred_element_type=F32))
    hres = x_ref[...] + g1_ref[...] * y
    hres_ref[...] = hres
    ms = jnp.mean(hres * hres, axis=-1, keepdims=True)
    h2 = hres * lax.rsqrt(ms + EPS) * nf_ref[...] * (1.0 + sc2_ref[...]) + sh2_ref[...]

    h2a_ref[...] = lax.bitcast_convert_type(h2[:, :HALF], U32)
    h2b_ref[...] = lax.bitcast_convert_type(h2[:, HALF:], U32)

    logits = jnp.dot(h2, wr_ref[...], precision=HIGHEST, preferred_element_type=F32) + br_ref[...]
    col = lax.broadcasted_iota(I32, logits.shape, 1)
    colf = col.astype(F32)
    lg = jnp.where(col < N_EXPERTS, logits, NEG_INF)
    vals, idxs = [], []
    for _ in range(TOP_K):
        mx = jnp.max(lg, axis=-1, keepdims=True)
        idx = jnp.min(jnp.where(lg == mx, colf, float(LANES)), axis=-1, keepdims=True)
        vals.append(mx)
        idxs.append(idx)
        lg = jnp.where(colf == idx, NEG_INF, lg)
    exps = [jnp.exp(vv - vals[0]) for vv in vals]
    denom = exps[0] + exps[1] + exps[2] + exps[3]
    sel = jnp.zeros(logits.shape, F32)
    for idx in idxs:
        sel = sel + (colf == idx).astype(F32)

    r = lax.broadcasted_iota(I32, (tm, tm), 0)
    cc = lax.broadcasted_iota(I32, (tm, tm), 1)
    tril = (cc <= r).astype(BF16)
    incl = jnp.dot(tril, sel.astype(BF16), preferred_element_type=F32) + carry_ref[...]
    carry_ref[...] = incl[tm - 1:tm, :]
    cnt_ref[...] = incl[tm - 1:tm, :]
    excl = incl - sel

    route = jnp.zeros(logits.shape, F32)
    for kk in range(TOP_K):
        rank = jnp.sum(jnp.where(colf == idxs[kk], excl, 0.0), axis=-1, keepdims=True)
        route = route + jnp.where(col == kk, idxs[kk], 0.0)
        route = route + jnp.where(col == TOP_K + kk, rank, 0.0)
        route = route + jnp.where(col == 2 * TOP_K + kk, exps[kk] / denom, 0.0)
    route_ref[...] = route


def _out_router(hm, hf, wt, wb, x2, g1, nf, sh2, sc2, wr, br):
    s, d = x2.shape
    tm = OUT_TM
    row = lambda i: (0, 0)
    return pl.pallas_call(
        _outrouter_kernel,
        grid=(s // tm,),
        in_specs=[pl.BlockSpec((tm, MLSTM_WIDTH), lambda i: (i, 0)),
                  pl.BlockSpec((tm, FOX_WIDTH), lambda i: (i, 0)),
                  pl.BlockSpec((MLSTM_WIDTH, d), row),
                  pl.BlockSpec((FOX_WIDTH, d), row),
                  pl.BlockSpec((tm, d), lambda i: (i, 0)),
                  pl.BlockSpec((1, d), row), pl.BlockSpec((1, d), row),
                  pl.BlockSpec((1, d), row), pl.BlockSpec((1, d), row),
                  pl.BlockSpec((d, LANES), row), pl.BlockSpec((1, LANES), row)],
        out_specs=[pl.BlockSpec((tm, d), lambda i: (i, 0)),
                   pl.BlockSpec((tm, HALF), lambda i: (i, 0)),
                   pl.BlockSpec((tm, HALF), lambda i: (i, 0)),
                   pl.BlockSpec((tm, LANES), lambda i: (i, 0)),
                   pl.BlockSpec((1, LANES), row)],
        out_shape=[jax.ShapeDtypeStruct((s, d), F32),
                   jax.ShapeDtypeStruct((s, HALF), U32),
                   jax.ShapeDtypeStruct((s, HALF), U32),
                   jax.ShapeDtypeStruct((s, LANES), F32),
                   jax.ShapeDtypeStruct((1, LANES), F32)],
        scratch_shapes=[pltpu.VMEM((1, LANES), F32)],
        compiler_params=_params(1, 48),
        name="out_router",
    )(hm, hf, wt, wb, x2, g1, nf, sh2, sc2, wr, br)


def _dispatch_kernel(zstart_ref, zflag_ref, dest_ref, h2_hbm, xs_hbm, zbuf, zsem, sem):
    i = pl.program_id(0)
    n = PAIRS_PER_STEP
    halves = ((h2_hbm, xs_hbm),)

    @pl.when(i == 0)
    def _():
        zbuf[...] = jnp.zeros_like(zbuf)

        def zero_copy(e, dst_hbm):
            z0 = pl.multiple_of(zstart_ref[e], ZERO_ROWS)
            return pltpu.make_async_copy(zbuf, dst_hbm.at[pl.ds(z0, ZERO_ROWS)], zsem)

        for e in range(N_EXPERTS):
            @pl.when(zflag_ref[e] == 1)
            def _():
                for _, dst_hbm in halves:
                    zero_copy(e, dst_hbm).start()
        for e in range(N_EXPERTS):
            @pl.when(zflag_ref[e] == 1)
            def _():
                for _, dst_hbm in halves:
                    zero_copy(e, dst_hbm).wait()

    def issue(j, _):
        tok = lax.shift_right_logical(i * n + j, 2)
        row = dest_ref[0, 0, j]
        for src_hbm, dst_hbm in halves:
            pltpu.make_async_copy(src_hbm.at[pl.ds(tok, 1)], dst_hbm.at[pl.ds(row, 1)], sem).start()
        return 0

    lax.fori_loop(0, n, issue, 0, unroll=8)
    for src_hbm, dst_hbm in halves:
        pltpu.make_async_copy(src_hbm.at[pl.ds(0, n)], dst_hbm.at[pl.ds(0, n)], sem).wait()


def _dispatch(zstart, zflag, dest3, h2_half, n_rows):
    steps = dest3.shape[0]
    any_spec = pl.BlockSpec(memory_space=pl.ANY)
    grid_spec = pltpu.PrefetchScalarGridSpec(
        num_scalar_prefetch=2,
        grid=(steps,),
        in_specs=[pl.BlockSpec((1, 1, PAIRS_PER_STEP), lambda i, zs, zf: (i, 0, 0), memory_space=pltpu.SMEM),
                  any_spec],
        out_specs=any_spec,
        scratch_shapes=[pltpu.VMEM((ZERO_ROWS, HALF), U32),
                        pltpu.SemaphoreType.DMA(()),
                        pltpu.SemaphoreType.DMA(())],
    )
    return pl.pallas_call(
        _dispatch_kernel,
        grid_spec=grid_spec,
        out_shape=jax.ShapeDtypeStruct((n_rows, HALF), U32),
        compiler_params=_params(1, 16),
        name="dispatch",
    )(zstart, zflag, dest3, h2_half)


def _expert_kernel(ce_ref, cn_ref, cmap_ref, nact_ref,
                   xa_ref, xb_ref, wg_ref, wu_ref, wd_ref, bg_ref, bu_ref, bd_ref, o_ref,
                   wgb_ref, wub_ref, wdb_ref):
    c = pl.program_id(0)
    f = pl.program_id(1)

    @pl.when(c < nact_ref[0])
    def _():
        @pl.when(f == 0)
        def _():
            o_ref[...] = jnp.broadcast_to(bd_ref[...], o_ref.shape)

        wgb_ref[...] = wg_ref[...].astype(BF16)
        wub_ref[...] = wu_ref[...].astype(BF16)
        wdb_ref[...] = wd_ref[...].astype(BF16)
        bg = bg_ref[...]
        bu = bu_ref[...]

        def sub_block(r0, rows):
            xa = lax.bitcast_convert_type(xa_ref[pl.ds(r0, rows), :], F32).astype(BF16)
            xb = lax.bitcast_convert_type(xb_ref[pl.ds(r0, rows), :], F32).astype(BF16)
            gt = (jnp.dot(xa, wgb_ref[:HALF, :], preferred_element_type=F32)
                  + jnp.dot(xb, wgb_ref[HALF:, :], preferred_element_type=F32) + bg)
            up = (jnp.dot(xa, wub_ref[:HALF, :], preferred_element_type=F32)
                  + jnp.dot(xb, wub_ref[HALF:, :], preferred_element_type=F32) + bu)
            gate = jnp.minimum(gt, SWIGLU_LIMIT)
            up = jnp.clip(up, -SWIGLU_LIMIT, SWIGLU_LIMIT)
            act = (up + 1.0) * gate * _sigmoid(SWIGLU_ALPHA * gate)
            o_ref[pl.ds(r0, rows), :] += jnp.dot(act.astype(BF16), wdb_ref[...], preferred_element_type=F32)

        n = cn_ref[c]
        shift = SUB_ROWS.bit_length() - 1
        n_full = lax.shift_right_logical(n, shift)
        rem = n & (SUB_ROWS - 1)
        n_big = n_full + (rem > SUB_ROWS // 2).astype(I32)

        def pair(p, _):
            r0 = pl.multiple_of(p * (2 * SUB_ROWS), 2 * SUB_ROWS)
            sub_block(r0, SUB_ROWS)
            sub_block(r0 + SUB_ROWS, SUB_ROWS)
            return 0

        lax.fori_loop(0, lax.shift_right_logical(n_big, 1), pair, 0)

        @pl.when((n_big & 1) == 1)
        def _():
            sub_block(pl.multiple_of((n_big - 1) * SUB_ROWS, SUB_ROWS), SUB_ROWS)

        @pl.when(jnp.logical_and(rem > 0, rem <= SUB_ROWS // 2))
        def _():
            sub_block(pl.multiple_of(n_full * SUB_ROWS, SUB_ROWS), SUB_ROWS // 2)


def _experts(ce, cn, cmap, nact, xsa, xsb, w_up_gate, w_down, b_up_gate3, b_down3, n_chunks):
    rc, tf = ROWS_PER_CHUNK, FF_TILE
    nf = D_FF // tf
    d = D_MODEL

    def fsel(c, f, nact_ref):
        return jnp.where(c < nact_ref[0], f, nf - 1)

    grid_spec = pltpu.PrefetchScalarGridSpec(
        num_scalar_prefetch=4,
        grid=(n_chunks, nf),
        in_specs=[
            pl.BlockSpec((rc, HALF), lambda c, f, ce, cn, cm, na: (cm[c], 0)),
            pl.BlockSpec((rc, HALF), lambda c, f, ce, cn, cm, na: (cm[c], 0)),
            pl.BlockSpec((None, d, tf), lambda c, f, ce, cn, cm, na: (ce[c], 0, fsel(c, f, na))),
            pl.BlockSpec((None, d, tf), lambda c, f, ce, cn, cm, na: (ce[c], 0, nf + fsel(c, f, na))),
            pl.BlockSpec((None, tf, d), lambda c, f, ce, cn, cm, na: (ce[c], fsel(c, f, na), 0)),
            pl.BlockSpec((None, 1, tf), lambda c, f, ce, cn, cm, na: (ce[c], 0, fsel(c, f, na))),
            pl.BlockSpec((None, 1, tf), lambda c, f, ce, cn, cm, na: (ce[c], 0, nf + fsel(c, f, na))),
            pl.BlockSpec((None, 1, d), lambda c, f, ce, cn, cm, na: (ce[c], 0, 0)),
        ],
        out_specs=pl.BlockSpec((rc, d), lambda c, f, ce, cn, cm, na: (cm[c], 0)),
        scratch_shapes=[pltpu.VMEM((d, tf), BF16), pltpu.VMEM((d, tf), BF16), pltpu.VMEM((tf, d), BF16)],
    )
    return pl.pallas_call(
        _expert_kernel,
        grid_spec=grid_spec,
        out_shape=jax.ShapeDtypeStruct((n_chunks * rc, d), F32),
        compiler_params=_params(2, 58),
        name="experts",
    )(ce, cn, cmap, nact, xsa, xsb, w_up_gate, w_up_gate, w_down, b_up_gate3, b_up_gate3, b_down3)


def _final_kernel(dcur_ref, dnext_ref, ys_hbm, hres_ref, route_ref, g2_ref, nf_ref, shf_ref, scf_ref, o_ref,
                  buf, sems):
    i = pl.program_id(0)
    tm = FINAL_TM
    slot = i & 1

    def issue(d_ref, dst_slot):
        def body(j, _):
            tok = lax.shift_right_logical(j, 2)
            k = j & (TOP_K - 1)
            pltpu.make_async_copy(ys_hbm.at[pl.ds(d_ref[0, 0, j], 1)],
                                  buf.at[dst_slot, k, pl.ds(tok, 1)], sems.at[dst_slot]).start()
            return 0
        lax.fori_loop(0, TOP_K * tm, body, 0, unroll=8)

    @pl.when(i == 0)
    def _():
        issue(dcur_ref, 0)

    @pl.when(i + 1 < pl.num_programs(0))
    def _():
        issue(dnext_ref, 1 - slot)

    for k in range(TOP_K):
        pltpu.make_async_copy(ys_hbm.at[pl.ds(0, tm)], buf.at[slot, k], sems.at[slot]).wait()

    route = route_ref[...]
    ffn = buf[slot, 0] * route[:, 2 * TOP_K:2 * TOP_K + 1]
    for k in range(1, TOP_K):
        ffn = ffn + buf[slot, k] * route[:, 2 * TOP_K + k:2 * TOP_K + k + 1]
    h = hres_ref[...] + g2_ref[...] * ffn
    ms = jnp.mean(h * h, axis=-1, keepdims=True)
    o_ref[...] = h * lax.rsqrt(ms + EPS) * nf_ref[...] * (1.0 + scf_ref[...]) + shf_ref[...]


def _final(dest3, ys, hres, route, g2, nf, shf, scf):
    s, d = hres.shape
    tm = FINAL_TM
    steps = s // tm
    row = lambda i: (0, 0)
    return pl.pallas_call(
        _final_kernel,
        grid=(steps,),
        in_specs=[pl.BlockSpec((1, 1, TOP_K * tm), lambda i: (i, 0, 0), memory_space=pltpu.SMEM),
                  pl.BlockSpec((1, 1, TOP_K * tm), lambda i: (jnp.minimum(i + 1, steps - 1), 0, 0),
                               memory_space=pltpu.SMEM),
                  pl.BlockSpec(memory_space=pl.ANY),
                  pl.BlockSpec((tm, d), lambda i: (i, 0)),
                  pl.BlockSpec((tm, LANES), lambda i: (i, 0)),
                  pl.BlockSpec((1, d), row), pl.BlockSpec((1, d), row),
                  pl.BlockSpec((1, d), row), pl.BlockSpec((1, d), row)],
        out_specs=pl.BlockSpec((tm, d), lambda i: (i, 0)),
        out_shape=jax.ShapeDtypeStruct((s, d), F32),
        scratch_shapes=[pltpu.VMEM((2, TOP_K, tm, d), F32), pltpu.SemaphoreType.DMA((2,))],
        compiler_params=_params(1, 40),
        name="final",
    )(dest3, dest3, ys, hres, route, g2, nf, shf, scf)


def kernel(x, c, w_ada, b_ada, norm_mix, w_in, b_i, b_f, fox_b_f, fox_q_norm, fox_k_norm, mlstm_out_norm,
           fox_out_norm, w_out, norm_ffn, w_router, b_router, w_up_gate, b_up_gate, w_down, b_down,
           w_ada_final, b_ada_final, norm_final):
    b, s, d = x.shape
    assert b == 1 and d == D_MODEL and w_ada.shape[0] == 1
    x2 = x.reshape(s, d).astype(F32)
    c_col = c.astype(F32).reshape(d, 1)

    mod = _ada_mod(c_col, w_ada[0], b_ada[0].reshape(1, -1))
    sh1, sc1, g1, sh2, sc2, g2 = [mod[:, i * d:(i + 1) * d] for i in range(6)]
    modf = _ada_mod(c_col, w_ada_final, b_ada_final.reshape(1, -1))
    shf, scf = modf[:, :d], modf[:, d:]

    w = w_in[0]
    o_gate = 2 * MLSTM_HEADS * MLSTM_DQK + 2 * MLSTM_WIDTH
    o_fq = o_gate + 2 * MLSTM_HEADS
    o_ff = o_fq + 3 * FOX_WIDTH
    wp = jnp.concatenate([w[:, :o_gate], w[:, o_fq:o_ff]], axis=1).astype(BF16)
    wg = jnp.concatenate([w[:, o_gate:o_fq], w[:, o_ff:o_ff + FOX_HEADS],
                          jnp.zeros((d, LANES - 16), F32)], axis=1).astype(BF16)
    qscale = MLSTM_DQK ** -0.5
    fscale = FOX_HEAD_DIM ** -0.5
    ones = lambda n: jnp.ones((n,), F32)
    colgain = jnp.concatenate([
        ones(MLSTM_HEADS * MLSTM_DQK) * qscale, ones(MLSTM_HEADS * MLSTM_DQK), ones(2 * MLSTM_WIDTH),
        jnp.tile(fox_q_norm[0].astype(F32), FOX_HEADS) * fscale, jnp.tile(fox_k_norm[0].astype(F32), FOX_HEADS),
        ones(FOX_WIDTH)]).reshape(1, P_WIDTH)

    p, gates = _in_proj(x2, norm_mix[0].reshape(1, d), sh1, sc1, wp, wg, colgain)

    gate_bias = jnp.concatenate([b_i[0], b_f[0], fox_b_f[0], jnp.zeros((LANES - 16,), F32)]).reshape(1, LANES)
    a = _gates(gates, gate_bias)
    at = a[:, :16].T

    hm = _mlstm(p, a, at, mlstm_out_norm[0].reshape(1, MLSTM_WIDTH).astype(F32))

    nq = s // FOX_BQ
    frow = at[2 * MLSTM_HEADS:2 * MLSTM_HEADS + FOX_HEADS]
    fref4 = jnp.broadcast_to(frow[:, ::FOX_BQ][:, :, None, None], (FOX_HEADS, nq, 1, FOX_BQ))
    hf = _fox_attn(p, frow.reshape(FOX_HEADS, 1, s), fref4,
                   fox_out_norm[0].reshape(FOX_HEADS, 1, FOX_HEAD_DIM).astype(F32))

    wo = w_out[0].astype(BF16)
    wr = jnp.concatenate([w_router[0].astype(F32), jnp.zeros((d, LANES - N_EXPERTS), F32)], axis=1)
    br = jnp.concatenate([b_router[0].astype(F32), jnp.zeros((LANES - N_EXPERTS,), F32)]).reshape(1, LANES)
    hres, h2a, h2b, route, cnt = _out_router(hm, hf, wo[:MLSTM_WIDTH], wo[MLSTM_WIDTH:], x2, g1,
                                        norm_ffn[0].reshape(1, d), sh2, sc2, wr, br)

    rc = ROWS_PER_CHUNK
    n_chunks = N_EXPERTS + (s * TOP_K) // rc
    counts = cnt[0, :N_EXPERTS].astype(I32)
    eidx = route[:, :TOP_K].astype(I32)
    rank = route[:, TOP_K:2 * TOP_K].astype(I32)
    nch = (counts + rc - 1) // rc
    cend = jnp.cumsum(nch)
    cstart = cend - nch
    nact = cend[-1]
    dest = cstart[eidx] * rc + rank
    cidx = jnp.arange(n_chunks, dtype=I32)
    cmap = jnp.minimum(cidx, nact - 1)
    ce = jnp.minimum(jnp.sum((cmap[:, None] >= cend[None, :]).astype(I32), axis=1), N_EXPERTS - 1)
    cn = jnp.where(cidx < nact, jnp.clip(counts[ce] - (cmap - cstart[ce]) * rc, 0, rc), 0).astype(I32)
    zflag = (counts % ZERO_ROWS != 0).astype(I32)
    zstart = (cstart * rc + (counts // ZERO_ROWS) * ZERO_ROWS).astype(I32)

    dest_steps = dest.reshape(-1, 1, PAIRS_PER_STEP)
    xsa = _dispatch(zstart, zflag, dest_steps, h2a, n_chunks * rc)
    xsb = _dispatch(zstart, zflag, dest_steps, h2b, n_chunks * rc)
    ys = _experts(ce, cn, cmap, nact.reshape(1).astype(I32), xsa, xsb, w_up_gate[0], w_down[0],
                  b_up_gate[0].reshape(N_EXPERTS, 1, 2 * D_FF), b_down[0].reshape(N_EXPERTS, 1, d), n_chunks)
    out = _final(dest.reshape(-1, 1, TOP_K * FINAL_TM), ys, hres, route, g2, norm_final.reshape(1, d), shf, scf)
    return out.reshape(b, s, d).astype(x.dtype)
```
